```python
import math
import jax, jax.numpy as jnp
from jax import lax
import numpy as np

D_MODEL = 2048
BATCH = 4
SEQ = 2048
DEPTH = 4

N_MIXERS = 4
MIX_WIDTH = D_MODEL
BRANCH_W = MIX_WIDTH // N_MIXERS
HEAD_DIM = 128
A_HEADS = BRANCH_W // HEAD_DIM
A_PATTERNS = ((128, 1), (512, 4), (2048, 16))
B_HEADS = BRANCH_W // HEAD_DIM
B_Q_RANK = 384
B_KV_RANK = 256
B_NOPE = 128
B_ROPE = 64
B_V = 128
C_HEADS = BRANCH_W // HEAD_DIM
MOBA_BLOCK = 256
MOBA_TOPK = 3
MOBA_QCHUNK = 32
D_HEAD_DIM = 64
D_HEADS = BRANCH_W // D_HEAD_DIM
D_KV_HEADS = 2
D_WINDOW = 128
BAND_BLOCK = 128
Q_BLOCK = 128
ROPE_THETA = 10000.0
RMS_EPS = 1e-6
LN_EPS = 1e-5
NEG = -1e30
DN_ALPHA = (2 * DEPTH) ** 0.25
DN_BETA = (8 * DEPTH) ** -0.25
SPLIT_SIZES = (BRANCH_W, BRANCH_W, BRANCH_W,
               B_Q_RANK, B_KV_RANK, B_ROPE,
               BRANCH_W, BRANCH_W, BRANCH_W,
               BRANCH_W, D_KV_HEADS * D_HEAD_DIM, D_KV_HEADS * D_HEAD_DIM,
               MIX_WIDTH)
N_IN = sum(SPLIT_SIZES)

kernel_name = "hybrid_dilated_mla_moba_sinkswa_deepnorm"


def rope(x, pos):
    dim = x.shape[-1]
    inv = ROPE_THETA ** (-jnp.arange(0, dim, 2, dtype=jnp.float32) / dim)
    ang = pos.astype(jnp.float32)[:, None] * inv[None, :]
    cos = jnp.concatenate([jnp.cos(ang), jnp.cos(ang)], -1)
    sin = jnp.concatenate([jnp.sin(ang), jnp.sin(ang)], -1)
    xf = x.astype(jnp.float32)
    x1, x2 = xf[..., : dim // 2], xf[..., dim // 2:]
    rot = jnp.concatenate([-x2, x1], -1)
    return (xf * cos + rot * sin).astype(x.dtype)


def rmsnorm(x, g):
    xf = x.astype(jnp.float32)
    y = xf * lax.rsqrt(jnp.mean(xf * xf, -1, keepdims=True) + RMS_EPS)
    return (y * g.astype(jnp.float32)).astype(x.dtype)


def layernorm(x, g, b):
    xf = x.astype(jnp.float32)
    mu = jnp.mean(xf, -1, keepdims=True)
    var = jnp.mean(jnp.square(xf - mu), -1, keepdims=True)
    y = (xf - mu) * lax.rsqrt(var + LN_EPS) * g.astype(jnp.float32) + b.astype(jnp.float32)
    return y.astype(x.dtype)


def heads(t, n):
    B, S, _ = t.shape
    return t.reshape(B, S, n, -1).transpose(0, 2, 1, 3)


def merge_heads(t):
    B, H, S, dh = t.shape
    return t.transpose(0, 2, 1, 3).reshape(B, S, H * dh)


def banded_window_stats(q, k, v, max_dist, scale):
    B, H, L, _ = q.shape
    dv = v.shape[-1]
    blk = BAND_BLOCK
    nb = -(-L // blk)
    Lp = nb * blk
    padw = ((0, 0), (0, 0), (0, Lp - L), (0, 0))
    qb = jnp.pad(q, padw).reshape(B, H, nb, blk, -1)
    kb = jnp.pad(k, padw).reshape(B, H, nb, blk, -1)
    vb = jnp.pad(v, padw).reshape(B, H, nb, blk, dv)

    def with_prev(t):
        prev = jnp.pad(t[:, :, :-1], ((0, 0), (0, 0), (1, 0), (0, 0), (0, 0)))
        return jnp.concatenate([prev, t], axis=3)

    kk, vv = with_prev(kb), with_prev(vb)
    s = jnp.einsum('bhnqd,bhnkd->bhnqk', qb, kk).astype(jnp.float32) * scale
    qi = jnp.arange(blk)[:, None] + blk
    kj = jnp.arange(2 * blk)[None, :]
    dist = qi - kj
    band = (dist >= 0) & (dist <= max_dist)
    has_prev = (jnp.arange(nb)[:, None, None] > 0) | (kj[None] >= blk)
    s = jnp.where(band[None] & has_prev, s, NEG)
    m = jnp.max(s, -1)
    p = jnp.exp(s - m[..., None])
    l = jnp.sum(p, -1)
    acc = jnp.einsum('bhnqk,bhnkd->bhnqd', p, vv.astype(jnp.float32))
    return (m.reshape(B, H, Lp)[:, :, :L], l.reshape(B, H, Lp)[:, :, :L],
            acc.reshape(B, H, Lp, dv)[:, :, :L])


def dilated_mixture_attention(q, k, v):
    B, H, S, dh = q.shape
    scale = dh ** -0.5
    ms, ls, accs = [], [], []
    for window, dil in A_PATTERNS:
        L = S // dil

        def gather(t):
            return t.reshape(B, H, L, dil, dh).transpose(0, 1, 3, 2, 4).reshape(B, H * dil, L, dh)

        m, l, acc = banded_window_stats(gather(q), gather(k), gather(v), window // dil, scale)
        ms.append(m.reshape(B, H, dil, L).transpose(0, 1, 3, 2).reshape(B, H, S))
        ls.append(l.reshape(B, H, dil, L).transpose(0, 1, 3, 2).reshape(B, H, S))
        accs.append(acc.reshape(B, H, dil, L, dh).transpose(0, 1, 3, 2, 4).reshape(B, H, S, dh))
    m_all = jnp.stack(ms)
    w = jnp.exp(m_all - jnp.max(m_all, 0, keepdims=True))
    num = jnp.sum(w[..., None] * jnp.stack(accs), 0)
    den = jnp.sum(w * jnp.stack(ls), 0)
    return num / den[..., None]


def causal_block_attention(q, k, v, scale):
    B, H, S, _ = q.shape
    kpos = jnp.arange(S)
    vf = v.astype(jnp.float32)

    def one_block(i):
        start = i * Q_BLOCK
        qb = lax.dynamic_slice_in_dim(q, start, Q_BLOCK, axis=2)
        s = jnp.einsum('bhqd,bhkd->bhqk', qb, k).astype(jnp.float32) * scale
        qpos = start + jnp.arange(Q_BLOCK)
        s = jnp.where(kpos[None, :] <= qpos[:, None], s, NEG)
        p = jax.nn.softmax(s, axis=-1)
        return jnp.einsum('bhqk,bhkd->bhqd', p, vf)

    o = lax.map(one_block, jnp.arange(S // Q_BLOCK))
    return o.transpose(1, 2, 0, 3, 4).reshape(B, H, S, -1)


def mla_attention(c_q, c_kv, k_rope, q_norm, w_uq, kv_norm, w_ukv, pos):
    B, S, _ = c_q.shape
    q = (rmsnorm(c_q, q_norm) @ w_uq).reshape(B, S, B_HEADS, B_NOPE + B_ROPE).transpose(0, 2, 1, 3)
    q = jnp.concatenate([q[..., :B_NOPE], rope(q[..., B_NOPE:], pos)], -1)
    kv = (rmsnorm(c_kv, kv_norm) @ w_ukv).reshape(B, S, B_HEADS, B_NOPE + B_V).transpose(0, 2, 1, 3)
    kr = jnp.broadcast_to(rope(k_rope, pos)[:, None], (B, B_HEADS, S, B_ROPE))
    k = jnp.concatenate([kv[..., :B_NOPE], kr], -1)
    v = kv[..., B_NOPE:]
    return causal_block_attention(q, k, v, (B_NOPE + B_ROPE) ** -0.5)


def moba_attention(q, k, v):
    B, H, S, dh = q.shape
    scale = dh ** -0.5
    nb = -(-S // MOBA_BLOCK)
    padw = ((0, 0), (0, 0), (0, nb * MOBA_BLOCK - S), (0, 0))
    kb = jnp.pad(k, padw).reshape(B, H, nb, MOBA_BLOCK, dh)
    vb = jnp.pad(v, padw).reshape(B, H, nb, MOBA_BLOCK, dh)
    kmean = jnp.mean(kb.astype(jnp.float32), axis=3)
    k_sel = min(MOBA_TOPK, nb)
    n_sel = k_sel * MOBA_BLOCK
    b_ix = jnp.arange(B)[:, None, None, None]
    h_ix = jnp.arange(H)[None, :, None, None]
    block_ids = jnp.arange(nb)

    def one_chunk(c):
        start = c * MOBA_QCHUNK
        n = start // MOBA_BLOCK
        qc = lax.dynamic_slice_in_dim(q, start, MOBA_QCHUNK, axis=2)
        gate = jnp.einsum('bhqd,bhnd->bhqn', qc.astype(jnp.float32), kmean)
        gate = jnp.where(block_ids < n, gate, NEG)
        _, idx = lax.top_k(gate, k_sel)
        sel_ok = jnp.repeat(idx < n, MOBA_BLOCK, axis=-1)
        kg = kb[b_ix, h_ix, idx].reshape(B, H, MOBA_QCHUNK, n_sel, dh)
        vg = vb[b_ix, h_ix, idx].reshape(B, H, MOBA_QCHUNK, n_sel, dh)
        s_sel = jnp.einsum('bhqd,bhqkd->bhqk', qc, kg).astype(jnp.float32) * scale
        s_sel = jnp.where(sel_ok, s_sel, NEG)
        k_own = lax.dynamic_index_in_dim(kb, n, axis=2, keepdims=False)
        v_own = lax.dynamic_index_in_dim(vb, n, axis=2, keepdims=False)
        s_own = jnp.einsum('bhqd,bhkd->bhqk', qc, k_own).astype(jnp.float32) * scale
        qpos = start + jnp.arange(MOBA_QCHUNK)
        kpos = n * MOBA_BLOCK + jnp.arange(MOBA_BLOCK)
        s_own = jnp.where(kpos[None, :] <= qpos[:, None], s_own, NEG)
        p = jax.nn.softmax(jnp.concatenate([s_sel, s_own], -1), axis=-1)
        return (jnp.einsum('bhqk,bhqkd->bhqd', p[..., :n_sel], vg.astype(jnp.float32))
                + jnp.einsum('bhqk,bhkd->bhqd', p[..., n_sel:], v_own.astype(jnp.float32)))

    o = lax.map(one_chunk, jnp.arange(S // MOBA_QCHUNK))
    return o.transpose(1, 2, 0, 3, 4).reshape(B, H, S, dh)


def sink_window_attention(q, k, v, sinks):
    rep = D_HEADS // D_KV_HEADS
    k = jnp.repeat(k, rep, axis=1)
    v = jnp.repeat(v, rep, axis=1)
    m, l, acc = banded_window_stats(q, k, v, D_WINDOW - 1, D_HEAD_DIM ** -0.5)
    sink = sinks.astype(jnp.float32)[None, :, None]
    m2 = jnp.maximum(m, sink)
    corr = jnp.exp(m - m2)
    den = l * corr + jnp.exp(sink - m2)
    return acc * (corr / den)[..., None]


def split_columns(proj):
    offs, acc = [], 0
    for s in SPLIT_SIZES[:-1]:
        acc += s
        offs.append(acc)
    return jnp.split(proj, offs, axis=-1)


def hybrid_layer(x, w_in, q_norm, w_uq, kv_norm, w_ukv, sinks, branch_norm, w_out, ln_g, ln_b, pos):
    B, S, _ = x.shape
    proj = x @ w_in
    (a_q, a_k, a_v, b_cq, b_ckv, b_kr, c_q, c_k, c_v, d_q, d_k, d_v, gate) = split_columns(proj)
    y_a = dilated_mixture_attention(rope(heads(a_q, A_HEADS), pos), rope(heads(a_k, A_HEADS), pos),
                                    heads(a_v, A_HEADS))
    y_b = mla_attention(b_cq, b_ckv, b_kr, q_norm, w_uq, kv_norm, w_ukv, pos)
    y_c = moba_attention(rope(heads(c_q, C_HEADS), pos), rope(heads(c_k, C_HEADS), pos),
                         heads(c_v, C_HEADS))
    y_d = sink_window_attention(rope(heads(d_q, D_HEADS), pos), rope(heads(d_k, D_KV_HEADS), pos),
                                heads(d_v, D_KV_HEADS), sinks)
    y = jnp.stack([merge_heads(y_a), merge_heads(y_b), merge_heads(y_c), merge_heads(y_d)], axis=2)
    y = y.astype(jnp.float32)
    y = y * lax.rsqrt(jnp.mean(y * y, -1, keepdims=True) + RMS_EPS)
    y = y.reshape(B, S, MIX_WIDTH) * branch_norm.astype(jnp.float32)
    y = (y * jax.nn.silu(gate.astype(jnp.float32))).astype(x.dtype)
    out = y @ w_out
    return layernorm(DN_ALPHA * x + out, ln_g, ln_b)


def setup_inputs(seed: int = 0) -> dict:
    key = jax.random.key(seed)
    ks = jax.random.split(key, 12)
    f32 = jnp.float32
    return {
        "x": jax.random.normal(ks[0], (BATCH, SEQ, D_MODEL), f32),
        "w_in": jax.random.normal(ks[1], (DEPTH, D_MODEL, N_IN), f32) * D_MODEL ** -0.5,
        "q_norm": 1.0 + 0.02 * jax.random.normal(ks[2], (DEPTH, B_Q_RANK), f32),
        "w_uq": jax.random.normal(ks[3], (DEPTH, B_Q_RANK, B_HEADS * (B_NOPE + B_ROPE)), f32) * B_Q_RANK ** -0.5,
        "kv_norm": 1.0 + 0.02 * jax.random.normal(ks[4], (DEPTH, B_KV_RANK), f32),
        "w_ukv": jax.random.normal(ks[5], (DEPTH, B_KV_RANK, B_HEADS * (B_NOPE + B_V)), f32) * B_KV_RANK ** -0.5,
        "sinks": 0.5 * jax.random.normal(ks[6], (DEPTH, D_HEADS), f32),
        "branch_norm": 1.0 + 0.02 * jax.random.normal(ks[7], (DEPTH, MIX_WIDTH), f32),
        "w_out": jax.random.normal(ks[8], (DEPTH, MIX_WIDTH, D_MODEL), f32) * (MIX_WIDTH ** -0.5 * DN_BETA),
        "ln_gamma": 1.0 + 0.02 * jax.random.normal(ks[9], (DEPTH, D_MODEL), f32),
        "ln_beta": 0.02 * jax.random.normal(ks[10], (DEPTH, D_MODEL), f32),
    }


def reference(x, w_in, q_norm, w_uq, kv_norm, w_ukv, sinks, branch_norm, w_out, ln_gamma, ln_beta):
    pos = jnp.arange(x.shape[1], dtype=jnp.int32)
    for l in range(DEPTH):
        x = hybrid_layer(x, w_in[l], q_norm[l], w_uq[l], kv_norm[l], w_ukv[l], sinks[l],
                         branch_norm[l], w_out[l], ln_gamma[l], ln_beta[l], pos)
    return x
```

```python
import functools
import math

import numpy as np
import jax
import jax.numpy as jnp
from jax import lax
from jax.experimental import pallas as pl
from jax.experimental.pallas import tpu as pltpu

F32 = jnp.float32
BF16 = jnp.bfloat16

D_MODEL = 2048
BATCH = 4
SEQ = 2048
DEPTH = 4
BRANCH_W = 512
HEAD_DIM = 128
A_PATTERNS = ((128, 1), (512, 4), (2048, 16))
B_HEADS = 4
B_Q_RANK = 384
B_KV_RANK = 256
B_NOPE = 128
B_ROPE = 64
B_V = 128
MOBA_BLOCK = 256
MOBA_TOPK = 3
D_HEAD_DIM = 64
D_HEADS = 8
D_KV_HEADS = 2
ROPE_THETA = 10000.0
RMS_EPS = 1e-6
LN_EPS = 1e-5
NEG = -1e30
DN_ALPHA = (2 * DEPTH) ** 0.25
SPLIT_SIZES = (512, 512, 512, 384, 256, 64, 512, 512, 512, 512, 128, 128, 2048)

LANE = 128
BLK = 128
ROWS = BATCH * SEQ

TN = 512
COL_AQ, COL_AK, COL_CQ, COL_CK, COL_DQ = 0, 512, 1024, 1536, 2048
COL_MIX = 2560
COL_DK, COL_KR, COL_CKV = 2560, 2688, 2816
COL_AV, COL_CV = 3072, 3584
COL_MIX2 = 4096
COL_BCQ, COL_DV = 4096, 4480
COL_GATE = 4608
N_PROJ = 6656

VMEM_LIMIT = 48 * 1024 * 1024


def _nt(a, b):
    return lax.dot_general(a, b, (((1,), (1,)), ((), ())), preferred_element_type=F32)


def _permute_w_in(w_in):
    offs = np.cumsum((0,) + SPLIT_SIZES)
    seg = [w_in[..., offs[i]:offs[i + 1]] for i in range(len(SPLIT_SIZES))]
    (a_q, a_k, a_v, b_cq, b_ckv, b_kr, c_q, c_k, c_v, d_q, d_k, d_v, gate) = seg
    pad = jnp.zeros(b_kr.shape[:-1] + (LANE - B_ROPE,), w_in.dtype)
    out = jnp.concatenate([a_q, a_k, c_q, c_k, d_q, d_k, b_kr, pad, b_ckv, a_v, c_v, b_cq, d_v, gate], -1)
    return out.astype(BF16)


def _permute_w_uq(w_uq):
    L = w_uq.shape[0]
    w = w_uq.reshape(L, B_Q_RANK, B_HEADS, B_NOPE + B_ROPE)
    w = jnp.pad(w, ((0, 0), (0, 0), (0, 0), (0, 2 * LANE - B_NOPE - B_ROPE)))
    return w.reshape(L, B_Q_RANK, B_HEADS * 2 * LANE).astype(BF16)


def _rope_tables():
    pos = np.arange(SEQ, dtype=np.float32).astype(np.float64)
    inv = (np.float32(ROPE_THETA) ** (-np.arange(0, 128, 2, dtype=np.float32) / np.float32(128))).astype(np.float64)
    ang = pos[:, None] * inv[None, :]
    c, s = np.cos(ang), np.sin(ang)
    cos128 = np.concatenate([c, c], -1)
    sin128 = np.concatenate([-s, s], -1)
    inv = (np.float32(ROPE_THETA) ** (-np.arange(0, 64, 2, dtype=np.float32) / np.float32(64))).astype(np.float64)
    ang = pos[:, None] * inv[None, :]
    c, s = np.cos(ang), np.sin(ang)
    z = np.zeros_like(s)
    cos64 = np.concatenate([c, c, c, c], -1)
    sin64a = np.concatenate([z, s, z, s], -1)
    sin64b = np.concatenate([-s, z, -s, z], -1)
    return tuple(jnp.asarray(t, F32) for t in (cos128, sin128, cos64, sin64a, sin64b))


TM_IN = 1024


def _rope128(x, c_ref, s_ref):
    return x * c_ref[...] + pltpu.roll(x, 64, 1) * s_ref[...]


def _rope64(x, c_ref, sa_ref, sb_ref):
    return x * c_ref[...] + pltpu.roll(x, 32, 1) * sa_ref[...] + pltpu.roll(x, 96, 1) * sb_ref[...]


def _inproj_body(x_ref, w_ref, c128, s128, c64, sa64, sb64, o_ref):
    j = pl.program_id(1)
    acc = jnp.dot(x_ref[...], w_ref[...], preferred_element_type=F32)

    @pl.when(j < 4)
    def _():
        scale = jnp.where((j == 0) | (j == 2), HEAD_DIM ** -0.5, 1.0).astype(F32)
        for g in range(TN // LANE):
            sl = slice(g * LANE, (g + 1) * LANE)
            o_ref[:, sl] = (_rope128(acc[:, sl], c128, s128) * scale).astype(BF16)

    @pl.when(j == 4)
    def _():
        for g in range(TN // LANE):
            sl = slice(g * LANE, (g + 1) * LANE)
            o_ref[:, sl] = (_rope64(acc[:, sl], c64, sa64, sb64) * (D_HEAD_DIM ** -0.5)).astype(BF16)

    @pl.when(j == 5)
    def _():
        for g in range(2):
            sl = slice(g * LANE, (g + 1) * LANE)
            o_ref[:, sl] = _rope64(acc[:, sl], c64, sa64, sb64).astype(BF16)
        o_ref[:, 2 * LANE:] = acc[:, 2 * LANE:].astype(BF16)

    @pl.when(j > 5)
    def _():
        o_ref[...] = acc.astype(BF16)


def _inproj(xb, w, tables):
    nt_pos = SEQ // TM_IN
    tab_spec = pl.BlockSpec((TM_IN, LANE), lambda i, j: (i % nt_pos, 0))
    return pl.pallas_call(
        _inproj_body,
        grid=(ROWS // TM_IN, N_PROJ // TN),
        in_specs=[pl.BlockSpec((TM_IN, D_MODEL), lambda i, j: (i, 0)),
                  pl.BlockSpec((D_MODEL, TN), lambda i, j: (0, j))] + [tab_spec] * 5,
        out_specs=pl.BlockSpec((TM_IN, TN), lambda i, j: (i, j)),
        out_shape=jax.ShapeDtypeStruct((ROWS, N_PROJ), BF16),
        compiler_params=pltpu.CompilerParams(dimension_semantics=("arbitrary", "arbitrary"),
                                             vmem_limit_bytes=VMEM_LIMIT),
        name="inproj",
    )(xb, w, *tables)


def _a_body(q_ref, k_ref, v_ref, o_ref, qf, kf, vf, m_s, l_s, acc_s):
    qf[...] = q_ref[...].astype(F32)
    kf[...] = k_ref[...].astype(F32)
    vf[...] = v_ref[...].astype(F32)
    row = lax.broadcasted_iota(jnp.int32, (BLK, BLK), 0)
    col = lax.broadcasted_iota(jnp.int32, (BLK, BLK), 1)
    own_ok = col <= row
    prev_ok = col >= row
    both_ok = jnp.concatenate([prev_ok, own_ok], axis=1)

    def rows(start, size, d):
        return pl.ds(start, size) if d == 1 else pl.ds(start, size, stride=d)

    def block(p, d, start, has_prev):
        q = qf[rows(start, BLK, d), :].astype(BF16)
        if has_prev:
            kk = kf[rows(start - d * BLK, 2 * BLK, d), :].astype(BF16)
            vv = vf[rows(start - d * BLK, 2 * BLK, d), :].astype(BF16)
            s = jnp.where(both_ok, _nt(q, kk), NEG)
        else:
            kk = kf[rows(start, BLK, d), :].astype(BF16)
            vv = vf[rows(start, BLK, d), :].astype(BF16)
            s = jnp.where(own_ok, _nt(q, kk), NEG)
        m = jnp.max(s, axis=-1, keepdims=True)
        e = jnp.exp(s - m)
        l = jnp.sum(e, axis=-1, keepdims=True)
        acc = jnp.dot(e.astype(BF16), vv, preferred_element_type=F32)
        dst = rows(start, BLK, d)
        m_s[p, dst, :] = jnp.broadcast_to(m, (BLK, LANE))
        l_s[p, dst, :] = jnp.broadcast_to(l, (BLK, LANE))
        acc_s[p, dst, :] = acc

    for p, (_, d) in enumerate(A_PATTERNS):
        nblk = SEQ // d // BLK
        for r in range(d):
            block(p, d, r, False)
            if nblk > 1:
                def body(n, carry, p=p, d=d, r=r):
                    block(p, d, r + d * BLK * n, True)
                    return carry
                lax.fori_loop(1, nblk, body, 0)

    CH = 256

    def combine(c, carry):
        sl = pl.ds(pl.multiple_of(c * CH, CH), CH)
        ms = [m_s[p, sl, :] for p in range(3)]
        mx = jnp.maximum(jnp.maximum(ms[0], ms[1]), ms[2])
        ws = [jnp.exp(m - mx) for m in ms]
        num = ws[0] * acc_s[0, sl, :] + ws[1] * acc_s[1, sl, :] + ws[2] * acc_s[2, sl, :]
        den = ws[0] * l_s[0, sl, :] + ws[1] * l_s[1, sl, :] + ws[2] * l_s[2, sl, :]
        o_ref[sl, :] = (num / den).astype(BF16)
        return carry

    lax.fori_loop(0, SEQ // CH, combine, 0)


def _a_attn(proj):
    nh = BRANCH_W // HEAD_DIM

    def spec(col):
        return pl.BlockSpec((SEQ, HEAD_DIM), lambda b, h: (b, col // HEAD_DIM + h))

    return pl.pallas_call(
        _a_body,
        grid=(BATCH, nh),
        in_specs=[spec(COL_AQ), spec(COL_AK), spec(COL_AV)],
        out_specs=pl.BlockSpec((SEQ, HEAD_DIM), lambda b, h: (b, h)),
        out_shape=jax.ShapeDtypeStruct((ROWS, BRANCH_W), BF16),
        scratch_shapes=[pltpu.VMEM((SEQ, HEAD_DIM), F32)] * 3 + [pltpu.VMEM((3, SEQ, LANE), F32)] * 3,
        compiler_params=pltpu.CompilerParams(dimension_semantics=("arbitrary", "arbitrary"),
                                             vmem_limit_bytes=VMEM_LIMIT),
        name="mixer_a",
    )(proj, proj, proj)


def _flash_t(q, k, v_t, mask, m, l, acc_t):
    s_t = _nt(k, q)
    if mask is not None:
        s_t = jnp.where(mask, s_t, NEG)
    m_new = jnp.maximum(m, jnp.max(s_t, axis=0, keepdims=True))
    alpha = jnp.exp(m - m_new)
    p_t = jnp.exp(s_t - m_new)
    l = alpha * l + jnp.sum(p_t, axis=0, keepdims=True)
    acc_t = alpha * acc_t + jnp.dot(v_t, p_t.astype(BF16), preferred_element_type=F32)
    return m_new, l, acc_t


QB = 256
NQB = SEQ // QB


def _causal_t():
    kk = lax.broadcasted_iota(jnp.int32, (QB, QB), 0)
    qq = lax.broadcasted_iota(jnp.int32, (QB, QB), 1)
    return kk <= qq


def _flash_init(dv):
    return (jnp.full((1, QB), NEG, F32), jnp.zeros((1, QB), F32), jnp.zeros((dv, QB), F32))


TM_BP = 512


def _rms(x, g_ref):
    xf = x.astype(F32)
    return xf * lax.rsqrt(jnp.mean(xf * xf, -1, keepdims=True) + RMS_EPS) * g_ref[...]


def _bprep_body(mix_ref, mix2_ref, qn_ref, wuq_ref, kvn_ref, wukv_ref, c64, sa64, sb64, q_ref, k_ref, v_ref):
    cq = _rms(mix2_ref[:, :B_Q_RANK], qn_ref).astype(BF16)
    q = jnp.dot(cq, wuq_ref[...], preferred_element_type=F32)
    scale = (B_NOPE + B_ROPE) ** -0.5
    for h in range(B_HEADS):
        o = h * 2 * LANE
        q_ref[:, o:o + LANE] = (q[:, o:o + LANE] * scale).astype(BF16)
        q_ref[:, o + LANE:o + 2 * LANE] = (_rope64(q[:, o + LANE:o + 2 * LANE], c64, sa64, sb64) * scale).astype(BF16)
    ckv = _rms(mix_ref[:, 2 * LANE:], kvn_ref).astype(BF16)
    kv = jnp.dot(ckv, wukv_ref[...], preferred_element_type=F32)
    kr = mix_ref[:, LANE:2 * LANE]
    for h in range(B_HEADS):
        o = h * 2 * LANE
        k_ref[:, o:o + LANE] = kv[:, o:o + LANE].astype(BF16)
        k_ref[:, o + LANE:o + 2 * LANE] = kr
        v_ref[:, h * LANE:(h + 1) * LANE] = kv[:, o + LANE:o + 2 * LANE].astype(BF16)


def _bprep(proj, q_norm, w_uq_p, kv_norm, w_ukv, tables):
    nt_pos = SEQ // TM_BP
    tab_spec = pl.BlockSpec((TM_BP, LANE), lambda i: (i % nt_pos, 0))
    full = lambda shape: pl.BlockSpec(shape, lambda i: (0, 0))
    return pl.pallas_call(
        _bprep_body,
        grid=(ROWS // TM_BP,),
        in_specs=[pl.BlockSpec((TM_BP, TN), lambda i: (i, COL_MIX // TN)),
                  pl.BlockSpec((TM_BP, TN), lambda i: (i, COL_MIX2 // TN)),
                  full((1, B_Q_RANK)), full((B_Q_RANK, B_HEADS * 2 * LANE)),
                  full((1, B_KV_RANK)), full((B_KV_RANK, B_HEADS * 2 * LANE))] + [tab_spec] * 3,
        out_specs=[pl.BlockSpec((TM_BP, B_HEADS * 2 * LANE), lambda i: (i, 0)),
                   pl.BlockSpec((TM_BP, B_HEADS * 2 * LANE), lambda i: (i, 0)),
                   pl.BlockSpec((TM_BP, B_HEADS * LANE), lambda i: (i, 0))],
        out_shape=[jax.ShapeDtypeStruct((ROWS, B_HEADS * 2 * LANE), BF16),
                   jax.ShapeDtypeStruct((ROWS, B_HEADS * 2 * LANE), BF16),
                   jax.ShapeDtypeStruct((ROWS, B_HEADS * LANE), BF16)],
        compiler_params=pltpu.CompilerParams(dimension_semantics=("arbitrary",),
                                             vmem_limit_bytes=VMEM_LIMIT),
        name="mla_prep",
    )(proj, proj, q_norm, w_uq_p, kv_norm, w_ukv, *tables[2:])


def _store_vt(v_ref, vt_s):
    for j in range(NQB):
        vt_s[j] = v_ref[j * QB:(j + 1) * QB, :].astype(F32).T.astype(BF16)


def _battn_body(q_ref, k_ref, v_ref, o_ref, vt_s):
    n = pl.program_id(2)

    @pl.when(n == 0)
    def _():
        _store_vt(v_ref, vt_s)

    q = q_ref[...]
    k_own = k_ref[pl.ds(pl.multiple_of(n * QB, QB), QB), :]
    carry = _flash_t(q, k_own, vt_s[n], _causal_t(), *_flash_init(B_V))

    def body(j, c):
        kj = k_ref[pl.ds(pl.multiple_of(j * QB, QB), QB), :]
        return _flash_t(q, kj, vt_s[j], None, *c)

    m, l, acc_t = lax.fori_loop(0, n, body, carry)
    o_ref[...] = (acc_t * (1.0 / l)).T.astype(BF16)


def _battn(qb, kb, vb):
    return pl.pallas_call(
        _battn_body,
        grid=(BATCH, B_HEADS, NQB),
        in_specs=[pl.BlockSpec((QB, 2 * LANE), lambda b, h, n: (b * NQB + n, h)),
                  pl.BlockSpec((SEQ, 2 * LANE), lambda b, h, n: (b, h)),
                  pl.BlockSpec((SEQ, B_V), lambda b, h, n: (b, h))],
        out_specs=pl.BlockSpec((QB, B_V), lambda b, h, n: (b * NQB + n, h)),
        out_shape=jax.ShapeDtypeStruct((ROWS, BRANCH_W), BF16),
        scratch_shapes=[pltpu.VMEM((NQB, B_V, QB), BF16)],
        compiler_params=pltpu.CompilerParams(dimension_semantics=("arbitrary",) * 3,
                                             vmem_limit_bytes=VMEM_LIMIT),
        name="mla_attn",
    )(qb, kb, vb)


KM_ROWS = 16


def _cattn_body(q_ref, k_ref, v_ref, o_ref, vt_s, km_s, sel_s):
    n = pl.program_id(2)

    @pl.when(n == 0)
    def _():
        _store_vt(v_ref, vt_s)
        km_s[...] = jnp.zeros_like(km_s)
        for j in range(NQB):
            kj = k_ref[j * QB:(j + 1) * QB, :].astype(F32)
            km_s[j:j + 1, :] = jnp.sum(kj, axis=0, keepdims=True) * (1.0 / MOBA_BLOCK)

    q = q_ref[...]
    km = km_s[...]
    km_hi = km.astype(BF16)
    km_lo = (km - km_hi.astype(F32)).astype(BF16)
    g_t = _nt(km_hi, q) + _nt(km_lo, q)
    jrow = lax.broadcasted_iota(jnp.int32, (KM_ROWS, QB), 0)
    cnt = jnp.zeros((KM_ROWS, QB), F32)
    for jp in range(NQB - 1):
        gj = g_t[jp:jp + 1, :]
        beats = (gj > g_t) | ((gj == g_t) & (jp < jrow))
        cnt = cnt + jnp.where(beats & (jp < n), 1.0, 0.0)
    sel_s[...] = jnp.where((jrow < n) & (cnt < MOBA_TOPK), 1.0, 0.0)

    k_own = k_ref[pl.ds(pl.multiple_of(n * QB, QB), QB), :]
    carry = _flash_t(q, k_own, vt_s[n], _causal_t(), *_flash_init(HEAD_DIM))

    def body(j, c):
        kj = k_ref[pl.ds(pl.multiple_of(j * QB, QB), QB), :]
        picked = sel_s[pl.ds(j, 1), :] > 0.5
        return _flash_t(q, kj, vt_s[j], picked, *c)

    m, l, acc_t = lax.fori_loop(0, n, body, carry)
    o_ref[...] = (acc_t * (1.0 / l)).T.astype(BF16)


def _cattn(proj):
    def kv_spec(col):
        return pl.BlockSpec((SEQ, HEAD_DIM), lambda b, h, n: (b, col // HEAD_DIM + h))

    return pl.pallas_call(
        _cattn_body,
        grid=(BATCH, BRANCH_W // HEAD_DIM, NQB),
        in_specs=[pl.BlockSpec((QB, HEAD_DIM), lambda b, h, n: (b * NQB + n, COL_CQ // HEAD_DIM + h)),
                  kv_spec(COL_CK), kv_spec(COL_CV)],
        out_specs=pl.BlockSpec((QB, HEAD_DIM), lambda b, h, n: (b * NQB + n, h)),
        out_shape=jax.ShapeDtypeStruct((ROWS, BRANCH_W), BF16),
        scratch_shapes=[pltpu.VMEM((NQB, HEAD_DIM, QB), BF16), pltpu.VMEM((KM_ROWS, HEAD_DIM), F32),
                        pltpu.VMEM((KM_ROWS, QB), F32)],
        compiler_params=pltpu.CompilerParams(dimension_semantics=("arbitrary",) * 3,
                                             vmem_limit_bytes=VMEM_LIMIT),
        name="moba_attn",
    )(proj, proj, proj)


D_REP = D_HEADS // D_KV_HEADS


def _dattn_body(sink_ref, q_ref, k_ref, v_ref, o_ref, kk_s, vv_s):
    lane_full = lax.broadcasted_iota(jnp.int32, (SEQ, LANE), 1) < D_HEAD_DIM
    kf = k_ref[...].astype(F32)
    vf = v_ref[...].astype(F32)
    kr = pltpu.roll(kf, D_HEAD_DIM, 1)
    vr = pltpu.roll(vf, D_HEAD_DIM, 1)
    kk_s[0] = jnp.where(lane_full, kf, kr).astype(BF16)
    kk_s[1] = jnp.where(lane_full, kr, kf).astype(BF16)
    vv_s[0] = jnp.where(lane_full, vf, vr).astype(BF16)
    vv_s[1] = jnp.where(lane_full, vr, vf).astype(BF16)

    lo = lax.broadcasted_iota(jnp.int32, (BLK, LANE), 1) < D_HEAD_DIM
    row = lax.broadcasted_iota(jnp.int32, (BLK, BLK), 0)
    col = lax.broadcasted_iota(jnp.int32, (BLK, BLK), 1)
    own_ok = jnp.concatenate([col <= row] * D_REP, axis=0)
    prev_ok = jnp.concatenate([col > row] * D_REP, axis=0)
    both_ok = jnp.concatenate([prev_ok, own_ok], axis=1)
    hrow = lax.broadcasted_iota(jnp.int32, (D_REP * BLK, 1), 0) // BLK

    for g in range(D_KV_HEADS):
        sink = jnp.zeros((D_REP * BLK, 1), F32)
        for i in range(D_REP):
            sink = jnp.where(hrow == i, sink_ref[g * D_REP + i], sink)

        def block(n, has_prev, g=g, sink=sink):
            r0 = pl.multiple_of(n * BLK, BLK)
            zero = jnp.zeros((BLK, LANE), BF16)
            parts = []
            for pg in range(2):
                c0 = (2 * g + pg) * LANE
                qp = q_ref[pl.ds(r0, BLK), c0:c0 + LANE]
                parts += [jnp.where(lo, qp, zero), jnp.where(lo, zero, qp)]
            q4 = jnp.concatenate(parts, axis=0)
            if has_prev:
                kv_rows = pl.ds(pl.multiple_of(r0 - BLK, BLK), 2 * BLK)
                s = jnp.where(both_ok, _nt(q4, kk_s[g, kv_rows, :]), NEG)
            else:
                kv_rows = pl.ds(r0, BLK)
                s = jnp.where(own_ok, _nt(q4, kk_s[g, kv_rows, :]), NEG)
            m2 = jnp.maximum(jnp.max(s, axis=-1, keepdims=True), sink)
            e = jnp.exp(s - m2)
            den = jnp.sum(e, axis=-1, keepdims=True) + jnp.exp(sink - m2)
            o4 = jnp.dot(e.astype(BF16), vv_s[g, kv_rows, :], preferred_element_type=F32) / den
            for pg in range(2):
                c0 = (2 * g + pg) * LANE
                o_ref[pl.ds(r0, BLK), c0:c0 + LANE] = jnp.where(
                    lo, o4[(2 * pg) * BLK:(2 * pg + 1) * BLK], o4[(2 * pg + 1) * BLK:(2 * pg + 2) * BLK]).astype(BF16)

        block(0, False)

        def body(n, carry, block=block):
            block(n, True)
            return carry

        lax.fori_loop(1, SEQ // BLK, body, 0)


def _dattn(proj, sinks):
    return pl.pallas_call(
        _dattn_body,
        grid=(BATCH,),
        in_specs=[pl.BlockSpec(memory_space=pltpu.SMEM),
                  pl.BlockSpec((SEQ, BRANCH_W), lambda b: (b, COL_DQ // BRANCH_W)),
                  pl.BlockSpec((SEQ, LANE), lambda b: (b, COL_DK // LANE)),
                  pl.BlockSpec((SEQ, LANE), lambda b: (b, COL_DV // LANE))],
        out_specs=pl.BlockSpec((SEQ, BRANCH_W), lambda b: (b, 0)),
        out_shape=jax.ShapeDtypeStruct((ROWS, BRANCH_W), BF16),
        scratch_shapes=[pltpu.VMEM((D_KV_HEADS, SEQ, LANE), BF16)] * 2,
        compiler_params=pltpu.CompilerParams(dimension_semantics=("arbitrary",),
                                             vmem_limit_bytes=VMEM_LIMIT),
        name="sink_swa",
    )(sinks, proj, proj, proj)


TM_OUT = 512


def _outproj_body(ya, yb, yc, yd, ga, gb, gc, gd, x_ref, w_ref, bn_ref, g_ref, b_ref, of_ref, ob_ref):
    acc = DN_ALPHA * x_ref[...]
    for i, (y_ref, gate_ref) in enumerate(((ya, ga), (yb, gb), (yc, gc), (yd, gd))):
        sl = slice(i * BRANCH_W, (i + 1) * BRANCH_W)
        y = y_ref[...].astype(F32)
        y = y * lax.rsqrt(jnp.mean(y * y, -1, keepdims=True) + RMS_EPS) * bn_ref[:, sl]
        gt = gate_ref[...].astype(F32)
        y = (y * (gt * jax.nn.sigmoid(gt))).astype(BF16)
        acc = acc + jnp.dot(y, w_ref[sl, :], preferred_element_type=F32)
    mu = jnp.mean(acc, -1, keepdims=True)
    xc = acc - mu
    var = jnp.mean(xc * xc, -1, keepdims=True)
    out = xc * lax.rsqrt(var + LN_EPS) * g_ref[...] + b_ref[...]
    of_ref[...] = out
    ob_ref[...] = out.astype(BF16)


def _outproj(ya, yb, yc, yd, proj, xf, w_out, bn, ln_g, ln_b):
    y_spec = pl.BlockSpec((TM_OUT, BRANCH_W), lambda i: (i, 0))
    gate_specs = [pl.BlockSpec((TM_OUT, BRANCH_W), functools.partial(lambda i, c: (i, c), c=COL_GATE // BRANCH_W + g))
                  for g in range(4)]
    row_spec = pl.BlockSpec((1, D_MODEL), lambda i: (0, 0))
    x_spec = pl.BlockSpec((TM_OUT, D_MODEL), lambda i: (i, 0))
    return pl.pallas_call(
        _outproj_body,
        grid=(ROWS // TM_OUT,),
        in_specs=[y_spec] * 4 + gate_specs + [x_spec, pl.BlockSpec((D_MODEL, D_MODEL), lambda i: (0, 0)),
                                              row_spec, row_spec, row_spec],
        out_specs=[x_spec, x_spec],
        out_shape=[jax.ShapeDtypeStruct((ROWS, D_MODEL), F32), jax.ShapeDtypeStruct((ROWS, D_MODEL), BF16)],
        compiler_params=pltpu.CompilerParams(dimension_semantics=("arbitrary",),
                                             vmem_limit_bytes=VMEM_LIMIT),
        name="outproj",
    )(ya, yb, yc, yd, proj, proj, proj, proj, xf, w_out, bn, ln_g, ln_b)


def kernel(x, w_in, q_norm, w_uq, kv_norm, w_ukv, sinks, branch_norm, w_out, ln_gamma, ln_beta):
    assert x.shape == (BATCH, SEQ, D_MODEL) and w_in.shape == (DEPTH, D_MODEL, sum(SPLIT_SIZES))
    tables = _rope_tables()
    w_in_p = _permute_w_in(w_in)
    w_uq_p = _permute_w_uq(w_uq)
    w_ukv_b = w_ukv.astype(BF16)
    w_out_b = w_out.astype(BF16)
    xf = x.reshape(ROWS, D_MODEL)
    xb = xf.astype(BF16)
    for l in range(DEPTH):
        proj = _inproj(xb, w_in_p[l], tables)
        ya = _a_attn(proj)
        qb, kb, vb = _bprep(proj, q_norm[l][None], w_uq_p[l], kv_norm[l][None], w_ukv_b[l], tables)
        yb = _battn(qb, kb, vb)
        yc = _cattn(proj)
        yd = _dattn(proj, sinks[l])
        xf, xb = _outproj(ya, yb, yc, yd, proj, xf, w_out_b[l], branch_norm[l][None], ln_gamma[l][None], ln_beta[l][None])
    return xf.reshape(BATCH, SEQ, D_MODEL)
```

```python
import functools
import math

import numpy as np
import jax
import jax.numpy as jnp
from jax import lax
from jax.experimental import pallas as pl
from jax.experimental.pallas import tpu as pltpu

F32 = jnp.float32
BF16 = jnp.bfloat16

D_MODEL = 2048
BATCH = 4
SEQ = 2048
DEPTH = 4
BRANCH_W = 512
HEAD_DIM = 128
A_PATTERNS = ((128, 1), (512, 4), (2048, 16))
B_HEADS = 4
B_Q_RANK = 384
B_KV_RANK = 256
B_NOPE = 128
B_ROPE = 64
B_V = 128
MOBA_BLOCK = 256
MOBA_TOPK = 3
D_HEAD_DIM = 64
D_HEADS = 8
D_KV_HEADS = 2
ROPE_THETA = 10000.0
RMS_EPS = 1e-6
LN_EPS = 1e-5
NEG = -1e30
DN_ALPHA = (2 * DEPTH) ** 0.25
SPLIT_SIZES = (512, 512, 512, 384, 256, 64, 512, 512, 512, 512, 128, 128, 2048)

LANE = 128
BLK = 128
ROWS = BATCH * SEQ

TN = 512
COL_AQ, COL_AK, COL_CQ, COL_CK, COL_DQ = 0, 512, 1024, 1536, 2048
COL_MIX = 2560
COL_DK, COL_KR, COL_CKV = 2560, 2688, 2816
COL_AV, COL_CV = 3072, 3584
COL_MIX2 = 4096
COL_BCQ, COL_DV = 4096, 4480
COL_GATE = 4608
N_PROJ = 6656

VMEM_LIMIT = 48 * 1024 * 1024


def _nt(a, b):
    return lax.dot_general(a, b, (((1,), (1,)), ((), ())), preferred_element_type=F32)


def _permute_w_in(w_in):
    offs = np.cumsum((0,) + SPLIT_SIZES)
    seg = [w_in[..., offs[i]:offs[i + 1]] for i in range(len(SPLIT_SIZES))]
    (a_q, a_k, a_v, b_cq, b_ckv, b_kr, c_q, c_k, c_v, d_q, d_k, d_v, gate) = seg
    pad = jnp.zeros(b_kr.shape[:-1] + (LANE - B_ROPE,), w_in.dtype)
    out = jnp.concatenate([a_q, a_k, c_q, c_k, d_q, d_k, b_kr, pad, b_ckv, a_v, c_v, b_cq, d_v, gate], -1)
    return out.astype(BF16)


def _permute_w_uq(w_uq):
    L = w_uq.shape[0]
    w = w_uq.reshape(L, B_Q_RANK, B_HEADS, B_NOPE + B_ROPE)
    w = jnp.pad(w, ((0, 0), (0, 0), (0, 0), (0, 2 * LANE - B_NOPE - B_ROPE)))
    return w.reshape(L, B_Q_RANK, B_HEADS * 2 * LANE).astype(BF16)


def _rope_tables():
    pos = np.arange(SEQ, dtype=np.float32).astype(np.float64)
    inv = (np.float32(ROPE_THETA) ** (-np.arange(0, 128, 2, dtype=np.float32) / np.float32(128))).astype(np.float64)
    ang = pos[:, None] * inv[None, :]
    c, s = np.cos(ang), np.sin(ang)
    cos128 = np.concatenate([c, c], -1)
    sin128 = np.concatenate([-s, s], -1)
    inv = (np.float32(ROPE_THETA) ** (-np.arange(0, 64, 2, dtype=np.float32) / np.float32(64))).astype(np.float64)
    ang = pos[:, None] * inv[None, :]
    c, s = np.cos(ang), np.sin(ang)
    z = np.zeros_like(s)
    cos64 = np.concatenate([c, c, c, c], -1)
    sin64a = np.concatenate([z, s, z, s], -1)
    sin64b = np.concatenate([-s, z, -s, z], -1)
    return tuple(jnp.asarray(t, F32) for t in (cos128, sin128, cos64, sin64a, sin64b))


TM_IN = 1024


def _rope128(x, c_ref, s_ref):
    return x * c_ref[...] + pltpu.roll(x, 64, 1) * s_ref[...]


def _rope64(x, c_ref, sa_ref, sb_ref):
    return x * c_ref[...] + pltpu.roll(x, 32, 1) * sa_ref[...] + pltpu.roll(x, 96, 1) * sb_ref[...]


def _inproj_body(x_ref, w_ref, c128, s128, c64, sa64, sb64, o_ref):
    j = pl.program_id(1)
    acc = jnp.dot(x_ref[...], w_ref[...], preferred_element_type=F32)

    @pl.when(j < 4)
    def _():
        scale = jnp.where((j == 0) | (j == 2), HEAD_DIM ** -0.5, 1.0).astype(F32)
        for g in range(TN // LANE):
            sl = slice(g * LANE, (g + 1) * LANE)
            o_ref[:, sl] = (_rope128(acc[:, sl], c128, s128) * scale).astype(BF16)

    @pl.when(j == 4)
    def _():
        for g in range(TN // LANE):
            sl = slice(g * LANE, (g + 1) * LANE)
            o_ref[:, sl] = (_rope64(acc[:, sl], c64, sa64, sb64) * (D_HEAD_DIM ** -0.5)).astype(BF16)

    @pl.when(j == 5)
    def _():
        for g in range(2):
            sl = slice(g * LANE, (g + 1) * LANE)
            o_ref[:, sl] = _rope64(acc[:, sl], c64, sa64, sb64).astype(BF16)
        o_ref[:, 2 * LANE:] = acc[:, 2 * LANE:].astype(BF16)

    @pl.when(j > 5)
    def _():
        o_ref[...] = acc.astype(BF16)


def _inproj(xb, w, tables):
    nt_pos = SEQ // TM_IN
    tab_spec = pl.BlockSpec((TM_IN, LANE), lambda i, j: (i % nt_pos, 0))
    return pl.pallas_call(
        _inproj_body,
        grid=(ROWS // TM_IN, N_PROJ // TN),
        in_specs=[pl.BlockSpec((TM_IN, D_MODEL), lambda i, j: (i, 0)),
                  pl.BlockSpec((D_MODEL, TN), lambda i, j: (0, j))] + [tab_spec] * 5,
        out_specs=pl.BlockSpec((TM_IN, TN), lambda i, j: (i, j)),
        out_shape=jax.ShapeDtypeStruct((ROWS, N_PROJ), BF16),
        compiler_params=pltpu.CompilerParams(dimension_semantics=("arbitrary", "arbitrary"),
                                             vmem_limit_bytes=VMEM_LIMIT),
        name="inproj",
    )(xb, w, *tables)


def _a_body(q_ref, k_ref, v_ref, o_ref, qf, kf, vf, m_s, l_s, acc_s):
    qf[...] = q_ref[...].astype(F32)
    kf[...] = k_ref[...].astype(F32)
    vf[...] = v_ref[...].astype(F32)
    row = lax.broadcasted_iota(jnp.int32, (BLK, BLK), 0)
    col = lax.broadcasted_iota(jnp.int32, (BLK, BLK), 1)
    own_ok = col <= row
    prev_ok = col >= row
    both_ok = jnp.concatenate([prev_ok, own_ok], axis=1)

    def rows(start, size, d):
        return pl.ds(start, size) if d == 1 else pl.ds(start, size, stride=d)

    def kv_rows(d, start, has_prev):
        return rows(start - d * BLK, 2 * BLK, d) if has_prev else rows(start, BLK, d)

    def scores(p, d, start, has_prev):
        q = qf[rows(start, BLK, d), :].astype(BF16)
        kk = kf[kv_rows(d, start, has_prev), :].astype(BF16)
        return jnp.where(both_ok if has_prev else own_ok, _nt(q, kk), NEG)

    def finish(s, p, d, start, has_prev):
        vv = vf[kv_rows(d, start, has_prev), :].astype(BF16)
        m = jnp.max(s, axis=-1, keepdims=True)
        e = jnp.exp(s - m)
        l = jnp.sum(e, axis=-1, keepdims=True)
        acc = jnp.dot(e.astype(BF16), vv, preferred_element_type=F32)
        dst = rows(start, BLK, d)
        m_s[p, dst, :] = jnp.broadcast_to(m, (BLK, LANE))
        l_s[p, dst, :] = jnp.broadcast_to(l, (BLK, LANE))
        acc_s[p, dst, :] = acc

    blocks = [(p, d, r + d * BLK * n, n > 0)
              for p, (_, d) in enumerate(A_PATTERNS) for r in range(d) for n in range(SEQ // d // BLK)]
    s_next = scores(*blocks[0])
    for i, blk in enumerate(blocks):
        s = s_next
        if i + 1 < len(blocks):
            s_next = scores(*blocks[i + 1])
        finish(s, *blk)

    CH = 256

    def combine(c, carry):
        sl = pl.ds(pl.multiple_of(c * CH, CH), CH)
        ms = [m_s[p, sl, :] for p in range(3)]
        mx = jnp.maximum(jnp.maximum(ms[0], ms[1]), ms[2])
        ws = [jnp.exp(m - mx) for m in ms]
        num = ws[0] * acc_s[0, sl, :] + ws[1] * acc_s[1, sl, :] + ws[2] * acc_s[2, sl, :]
        den = ws[0] * l_s[0, sl, :] + ws[1] * l_s[1, sl, :] + ws[2] * l_s[2, sl, :]
        o_ref[sl, :] = (num / den).astype(BF16)
        return carry

    lax.fori_loop(0, SEQ // CH, combine, 0)


def _a_attn(proj):
    nh = BRANCH_W // HEAD_DIM

    def spec(col):
        return pl.BlockSpec((SEQ, HEAD_DIM), lambda b, h: (b, col // HEAD_DIM + h))

    return pl.pallas_call(
        _a_body,
        grid=(BATCH, nh),
        in_specs=[spec(COL_AQ), spec(COL_AK), spec(COL_AV)],
        out_specs=pl.BlockSpec((SEQ, HEAD_DIM), lambda b, h: (b, h)),
        out_shape=jax.ShapeDtypeStruct((ROWS, BRANCH_W), BF16),
        scratch_shapes=[pltpu.VMEM((SEQ, HEAD_DIM), F32)] * 3 + [pltpu.VMEM((3, SEQ, LANE), F32)] * 3,
        compiler_params=pltpu.CompilerParams(dimension_semantics=("arbitrary", "arbitrary"),
                                             vmem_limit_bytes=VMEM_LIMIT),
        name="mixer_a",
    )(proj, proj, proj)


QB = 256
NQB = SEQ // QB


def _rows(ref, i):
    return ref[i * QB:(i + 1) * QB, :]


def _flash_update(s_t, v_t, m, l, acc_t):
    m_new = jnp.maximum(m, jnp.max(s_t, axis=0, keepdims=True))
    alpha = jnp.exp(m - m_new)
    p_t = jnp.exp(s_t - m_new)
    l = alpha * l + jnp.sum(p_t, axis=0, keepdims=True)
    acc_t = alpha * acc_t + jnp.dot(v_t, p_t.astype(BF16), preferred_element_type=F32)
    return m_new, l, acc_t


def _causal_attend(q_ref, k_ref, vt_s, o_ref, dv, bias_of):
    kk = lax.broadcasted_iota(jnp.int32, (QB, QB), 0)
    qq = lax.broadcasted_iota(jnp.int32, (QB, QB), 1)
    causal = kk <= qq
    pairs = [(n, j) for n in range(NQB) for j in [n] + list(range(n))]

    def scores(n, j):
        return _nt(_rows(k_ref, j), _rows(q_ref, n))

    s_next = scores(*pairs[0])
    state = None
    for i, (n, j) in enumerate(pairs):
        s_t = s_next
        if i + 1 < len(pairs):
            s_next = scores(*pairs[i + 1])
        if j == n:
            state = (jnp.full((1, QB), NEG, F32), jnp.zeros((1, QB), F32), jnp.zeros((dv, QB), F32))
            s_t = jnp.where(causal, s_t, NEG)
        else:
            bias = bias_of(n, j)
            if bias is not None:
                s_t = s_t + bias
        state = _flash_update(s_t, vt_s[j], *state)
        if j == max(n - 1, 0):
            m, l, acc_t = state
            o_ref[n * QB:(n + 1) * QB, :] = (acc_t * (1.0 / l)).T.astype(BF16)


TM_BP = 512


def _rms(x, g_ref):
    xf = x.astype(F32)
    return xf * lax.rsqrt(jnp.mean(xf * xf, -1, keepdims=True) + RMS_EPS) * g_ref[...]


def _bprep_body(mix_ref, mix2_ref, qn_ref, wuq_ref, kvn_ref, wukv_ref, c64, sa64, sb64, q_ref, k_ref, v_ref):
    cq = _rms(mix2_ref[:, :B_Q_RANK], qn_ref).astype(BF16)
    q = jnp.dot(cq, wuq_ref[...], preferred_element_type=F32)
    scale = (B_NOPE + B_ROPE) ** -0.5
    for h in range(B_HEADS):
        o = h * 2 * LANE
        q_ref[:, o:o + LANE] = (q[:, o:o + LANE] * scale).astype(BF16)
        q_ref[:, o + LANE:o + 2 * LANE] = (_rope64(q[:, o + LANE:o + 2 * LANE], c64, sa64, sb64) * scale).astype(BF16)
    ckv = _rms(mix_ref[:, 2 * LANE:], kvn_ref).astype(BF16)
    kv = jnp.dot(ckv, wukv_ref[...], preferred_element_type=F32)
    kr = mix_ref[:, LANE:2 * LANE]
    for h in range(B_HEADS):
        o = h * 2 * LANE
        k_ref[:, o:o + LANE] = kv[:, o:o + LANE].astype(BF16)
        k_ref[:, o + LANE:o + 2 * LANE] = kr
        v_ref[:, h * LANE:(h + 1) * LANE] = kv[:, o + LANE:o + 2 * LANE].astype(BF16)


def _bprep(proj, q_norm, w_uq_p, kv_norm, w_ukv, tables):
    nt_pos = SEQ // TM_BP
    tab_spec = pl.BlockSpec((TM_BP, LANE), lambda i: (i % nt_pos, 0))
    full = lambda shape: pl.BlockSpec(shape, lambda i: (0, 0))
    return pl.pallas_call(
        _bprep_body,
        grid=(ROWS // TM_BP,),
        in_specs=[pl.BlockSpec((TM_BP, TN), lambda i: (i, COL_MIX // TN)),
                  pl.BlockSpec((TM_BP, TN), lambda i: (i, COL_MIX2 // TN)),
                  full((1, B_Q_RANK)), full((B_Q_RANK, B_HEADS * 2 * LANE)),
                  full((1, B_KV_RANK)), full((B_KV_RANK, B_HEADS * 2 * LANE))] + [tab_spec] * 3,
        out_specs=[pl.BlockSpec((TM_BP, B_HEADS * 2 * LANE), lambda i: (i, 0)),
                   pl.BlockSpec((TM_BP, B_HEADS * 2 * LANE), lambda i: (i, 0)),
                   pl.BlockSpec((TM_BP, B_HEADS * LANE), lambda i: (i, 0))],
        out_shape=[jax.ShapeDtypeStruct((ROWS, B_HEADS * 2 * LANE), BF16),
                   jax.ShapeDtypeStruct((ROWS, B_HEADS * 2 * LANE), BF16),
                   jax.ShapeDtypeStruct((ROWS, B_HEADS * LANE), BF16)],
        compiler_params=pltpu.CompilerParams(dimension_semantics=("arbitrary",),
                                             vmem_limit_bytes=VMEM_LIMIT),
        name="mla_prep",
    )(proj, proj, q_norm, w_uq_p, kv_norm, w_ukv, *tables[2:])


def _store_vt(v_ref, vt_s):
    for j in range(NQB):
        vt_s[j] = v_ref[j * QB:(j + 1) * QB, :].astype(F32).T.astype(BF16)


def _battn_body(q_ref, k_ref, v_ref, o_ref, vt_s):
    _store_vt(v_ref, vt_s)
    _causal_attend(q_ref, k_ref, vt_s, o_ref, B_V, lambda n, j: None)


def _battn(qb, kb, vb):
    return pl.pallas_call(
        _battn_body,
        grid=(BATCH, B_HEADS),
        in_specs=[pl.BlockSpec((SEQ, 2 * LANE), lambda b, h: (b, h)),
                  pl.BlockSpec((SEQ, 2 * LANE), lambda b, h: (b, h)),
                  pl.BlockSpec((SEQ, B_V), lambda b, h: (b, h))],
        out_specs=pl.BlockSpec((SEQ, B_V), lambda b, h: (b, h)),
        out_shape=jax.ShapeDtypeStruct((ROWS, BRANCH_W), BF16),
        scratch_shapes=[pltpu.VMEM((NQB, B_V, QB), BF16)],
        compiler_params=pltpu.CompilerParams(dimension_semantics=("arbitrary",) * 2,
                                             vmem_limit_bytes=VMEM_LIMIT),
        name="mla_attn",
    )(qb, kb, vb)


KM_ROWS = 16


def _cattn_body(q_ref, k_ref, v_ref, o_ref, vt_s, km_s, bias_s):
    _store_vt(v_ref, vt_s)
    km_s[...] = jnp.zeros_like(km_s)
    for j in range(NQB):
        km_s[j:j + 1, :] = jnp.sum(_rows(k_ref, j).astype(F32), axis=0, keepdims=True) * (1.0 / MOBA_BLOCK)
    km = km_s[...]
    km_hi = km.astype(BF16)
    km_lo = (km - km_hi.astype(F32)).astype(BF16)
    jrow = lax.broadcasted_iota(jnp.int32, (KM_ROWS, QB), 0)
    for n in range(MOBA_TOPK + 1, NQB):
        q = _rows(q_ref, n)
        g_t = _nt(km_hi, q) + _nt(km_lo, q)
        cnt = jnp.zeros((KM_ROWS, QB), F32)
        for jp in range(n):
            gj = g_t[jp:jp + 1, :]
            beats = (gj > g_t) | ((gj == g_t) & (jp < jrow))
            cnt = cnt + jnp.where(beats, 1.0, 0.0)
        bias_s[n] = jnp.where(cnt < MOBA_TOPK, 0.0, NEG)

    def bias_of(n, j):
        return bias_s[n, j:j + 1, :] if n > MOBA_TOPK else None

    _causal_attend(q_ref, k_ref, vt_s, o_ref, HEAD_DIM, bias_of)


def _cattn(proj):
    def spec(col):
        return pl.BlockSpec((SEQ, HEAD_DIM), lambda b, h: (b, col // HEAD_DIM + h))

    return pl.pallas_call(
        _cattn_body,
        grid=(BATCH, BRANCH_W // HEAD_DIM),
        in_specs=[spec(COL_CQ), spec(COL_CK), spec(COL_CV)],
        out_specs=pl.BlockSpec((SEQ, HEAD_DIM), lambda b, h: (b, h)),
        out_shape=jax.ShapeDtypeStruct((ROWS, BRANCH_W), BF16),
        scratch_shapes=[pltpu.VMEM((NQB, HEAD_DIM, QB), BF16), pltpu.VMEM((KM_ROWS, HEAD_DIM), F32),
                        pltpu.VMEM((NQB, KM_ROWS, QB), F32)],
        compiler_params=pltpu.CompilerParams(dimension_semantics=("arbitrary",) * 2,
                                             vmem_limit_bytes=VMEM_LIMIT),
        name="moba_attn",
    )(proj, proj, proj)


D_REP = D_HEADS // D_KV_HEADS


def _dattn_body(sink_ref, q_ref, k_ref, v_ref, o_ref, kk_s, vv_s):
    lane_full = lax.broadcasted_iota(jnp.int32, (SEQ, LANE), 1) < D_HEAD_DIM
    kf = k_ref[...].astype(F32)
    vf = v_ref[...].astype(F32)
    kr = pltpu.roll(kf, D_HEAD_DIM, 1)
    vr = pltpu.roll(vf, D_HEAD_DIM, 1)
    kk_s[0] = jnp.where(lane_full, kf, kr).astype(BF16)
    kk_s[1] = jnp.where(lane_full, kr, kf).astype(BF16)
    vv_s[0] = jnp.where(lane_full, vf, vr).astype(BF16)
    vv_s[1] = jnp.where(lane_full, vr, vf).astype(BF16)

    lo = lax.broadcasted_iota(jnp.int32, (BLK, LANE), 1) < D_HEAD_DIM
    row = lax.broadcasted_iota(jnp.int32, (BLK, BLK), 0)
    col = lax.broadcasted_iota(jnp.int32, (BLK, BLK), 1)
    own_ok = jnp.concatenate([col <= row] * D_REP, axis=0)
    prev_ok = jnp.concatenate([col > row] * D_REP, axis=0)
    both_ok = jnp.concatenate([prev_ok, own_ok], axis=1)
    hrow = lax.broadcasted_iota(jnp.int32, (D_REP * BLK, 1), 0) // BLK

    sinks = []
    for g in range(D_KV_HEADS):
        sink = jnp.zeros((D_REP * BLK, 1), F32)
        for i in range(D_REP):
            sink = jnp.where(hrow == i, sink_ref[g * D_REP + i], sink)
        sinks.append(sink)

    def kv_rows(n):
        return pl.ds((n - 1) * BLK, 2 * BLK) if n > 0 else pl.ds(0, BLK)

    def scores(g, n):
        zero = jnp.zeros((BLK, LANE), BF16)
        parts = []
        for pg in range(2):
            c0 = (2 * g + pg) * LANE
            qp = q_ref[n * BLK:(n + 1) * BLK, c0:c0 + LANE]
            parts += [jnp.where(lo, qp, zero), jnp.where(lo, zero, qp)]
        q4 = jnp.concatenate(parts, axis=0)
        return jnp.where(both_ok if n > 0 else own_ok, _nt(q4, kk_s[g, kv_rows(n), :]), NEG)

    def finish(s, g, n):
        m2 = jnp.maximum(jnp.max(s, axis=-1, keepdims=True), sinks[g])
        e = jnp.exp(s - m2)
        den = jnp.sum(e, axis=-1, keepdims=True) + jnp.exp(sinks[g] - m2)
        o4 = jnp.dot(e.astype(BF16), vv_s[g, kv_rows(n), :], preferred_element_type=F32) / den
        for pg in range(2):
            c0 = (2 * g + pg) * LANE
            o_ref[n * BLK:(n + 1) * BLK, c0:c0 + LANE] = jnp.where(
                lo, o4[(2 * pg) * BLK:(2 * pg + 1) * BLK], o4[(2 * pg + 1) * BLK:(2 * pg + 2) * BLK]).astype(BF16)

    blocks = [(g, n) for g in range(D_KV_HEADS) for n in range(SEQ // BLK)]
    s_next = scores(*blocks[0])
    for i, blk in enumerate(blocks):
        s = s_next
        if i + 1 < len(blocks):
            s_next = scores(*blocks[i + 1])
        finish(s, *blk)


def _dattn(proj, sinks):
    return pl.pallas_call(
        _dattn_body,
        grid=(BATCH,),
        in_specs=[pl.BlockSpec(memory_space=pltpu.SMEM),
                  pl.BlockSpec((SEQ, BRANCH_W), lambda b: (b, COL_DQ // BRANCH_W)),
                  pl.BlockSpec((SEQ, LANE), lambda b: (b, COL_DK // LANE)),
                  pl.BlockSpec((SEQ, LANE), lambda b: (b, COL_DV // LANE))],
        out_specs=pl.BlockSpec((SEQ, BRANCH_W), lambda b: (b, 0)),
        out_shape=jax.ShapeDtypeStruct((ROWS, BRANCH_W), BF16),
        scratch_shapes=[pltpu.VMEM((D_KV_HEADS, SEQ, LANE), BF16)] * 2,
        compiler_params=pltpu.CompilerParams(dimension_semantics=("arbitrary",),
                                             vmem_limit_bytes=VMEM_LIMIT),
        name="sink_swa",
    )(sinks, proj, proj, proj)


TM_OUT = 512


def _outproj_body(ya, yb, yc, yd, ga, gb, gc, gd, x_ref, w_ref, bn_ref, g_ref, b_ref, of_ref, ob_ref):
    acc = DN_ALPHA * x_ref[...]
    for i, (y_ref, gate_ref) in enumerate(((ya, ga), (yb, gb), (yc, gc), (yd, gd))):
        sl = slice(i * BRANCH_W, (i + 1) * BRANCH_W)
        y = y_ref[...].astype(F32)
        y = y * lax.rsqrt(jnp.mean(y * y, -1, keepdims=True) + RMS_EPS) * bn_ref[:, sl]
        gt = gate_ref[...].astype(F32)
        y = (y * (gt * jax.nn.sigmoid(gt))).astype(BF16)
        acc = acc + jnp.dot(y, w_ref[sl, :], preferred_element_type=F32)
    mu = jnp.mean(acc, -1, keepdims=True)
    xc = acc - mu
    var = jnp.mean(xc * xc, -1, keepdims=True)
    out = xc * lax.rsqrt(var + LN_EPS) * g_ref[...] + b_ref[...]
    of_ref[...] = out
    ob_ref[...] = out.astype(BF16)


def _outproj(ya, yb, yc, yd, proj, xf, w_out, bn, ln_g, ln_b):
    y_spec = pl.BlockSpec((TM_OUT, BRANCH_W), lambda i: (i, 0))
    gate_specs = [pl.BlockSpec((TM_OUT, BRANCH_W), functools.partial(lambda i, c: (i, c), c=COL_GATE // BRANCH_W + g))
                  for g in range(4)]
    row_spec = pl.BlockSpec((1, D_MODEL), lambda i: (0, 0))
    x_spec = pl.BlockSpec((TM_OUT, D_MODEL), lambda i: (i, 0))
    return pl.pallas_call(
        _outproj_body,
        grid=(ROWS // TM_OUT,),
        in_specs=[y_spec] * 4 + gate_specs + [x_spec, pl.BlockSpec((D_MODEL, D_MODEL), lambda i: (0, 0)),
                                              row_spec, row_spec, row_spec],
        out_specs=[x_spec, x_spec],
        out_shape=[jax.ShapeDtypeStruct((ROWS, D_MODEL), F32), jax.ShapeDtypeStruct((ROWS, D_MODEL), BF16)],
        compiler_params=pltpu.CompilerParams(dimension_semantics=("arbitrary",),
                                             vmem_limit_bytes=VMEM_LIMIT),
        name="outproj",
    )(ya, yb, yc, yd, proj, proj, proj, proj, xf, w_out, bn, ln_g, ln_b)


def kernel(x, w_in, q_norm, w_uq, kv_norm, w_ukv, sinks, branch_norm, w_out, ln_gamma, ln_beta):
    assert x.shape == (BATCH, SEQ, D_MODEL) and w_in.shape == (DEPTH, D_MODEL, sum(SPLIT_SIZES))
    tables = _rope_tables()
    w_in_p = _permute_w_in(w_in)
    w_uq_p = _permute_w_uq(w_uq)
    w_ukv_b = w_ukv.astype(BF16)
    w_out_b = w_out.astype(BF16)
    xf = x.reshape(ROWS, D_MODEL)
    xb = xf.astype(BF16)
    for l in range(DEPTH):
        proj = _inproj(xb, w_in_p[l], tables)
        ya = _a_attn(proj)
        qb, kb, vb = _bprep(proj, q_norm[l][None], w_uq_p[l], kv_norm[l][None], w_ukv_b[l], tables)
        yb = _battn(qb, kb, vb)
        yc = _cattn(proj)
        yd = _dattn(proj, sinks[l])
        xf, xb = _outproj(ya, yb, yc, yd, proj, xf, w_out_b[l], branch_norm[l][None], ln_gamma[l][None], ln_beta[l][None])
    return xf.reshape(BATCH, SEQ, D_MODEL)
```

```python
import functools
import math

import numpy as np
import jax
import jax.numpy as jnp
from jax import lax
from jax.experimental import pallas as pl
from jax.experimental.pallas import tpu as pltpu

F32 = jnp.float32
BF16 = jnp.bfloat16

D_MODEL = 2048
BATCH = 4
SEQ = 2048
DEPTH = 4
BRANCH_W = 512
HEAD_DIM = 128
A_PATTERNS = ((128, 1), (512, 4), (2048, 16))
B_HEADS = 4
B_Q_RANK = 384
B_KV_RANK = 256
B_NOPE = 128
B_ROPE = 64
B_V = 128
MOBA_BLOCK = 256
MOBA_TOPK = 3
D_HEAD_DIM = 64
D_HEADS = 8
D_KV_HEADS = 2
ROPE_THETA = 10000.0
RMS_EPS = 1e-6
LN_EPS = 1e-5
NEG = -1e30
DN_ALPHA = (2 * DEPTH) ** 0.25
SPLIT_SIZES = (512, 512, 512, 384, 256, 64, 512, 512, 512, 512, 128, 128, 2048)

LANE = 128
BLK = 128
ROWS = BATCH * SEQ

TN = 512
COL_AQ, COL_AK, COL_CQ, COL_CK, COL_DQ = 0, 512, 1024, 1536, 2048
COL_MIX = 2560
COL_DK, COL_KR, COL_CKV = 2560, 2688, 2816
COL_AV, COL_CV = 3072, 3584
COL_MIX2 = 4096
COL_BCQ, COL_DV = 4096, 4480
COL_GATE = 4608
N_PROJ = 6656

VMEM_LIMIT = 48 * 1024 * 1024


def _nt(a, b):
    return lax.dot_general(a, b, (((1,), (1,)), ((), ())), preferred_element_type=F32)


N_SRC = 5
_OFFS = np.cumsum((0,) + SPLIT_SIZES)
(_O_AQ, _O_AK, _O_AV, _O_BCQ, _O_BCKV, _O_BKR, _O_CQ, _O_CK, _O_CV, _O_DQ, _O_DK, _O_DV, _O_GATE) = (int(o) for o in _OFFS[:-1])


def _src_blocks():
    def run(start):
        b = start // LANE
        return [min(b + s, (int(_OFFS[-1]) - 1) // LANE) for s in range(N_SRC)]

    tiles = [run(_O_AQ), run(_O_AK), run(_O_CQ), run(_O_CK), run(_O_DQ),
             [_O_DK // LANE, _O_DK // LANE + 1, _O_BKR // LANE, _O_BCKV // LANE, _O_BCKV // LANE + 1],
             run(_O_AV), run(_O_CV),
             [_O_BCQ // LANE, _O_BCQ // LANE + 1, _O_BCQ // LANE + 2, _O_DV // LANE, _O_DV // LANE + 1]]
    tiles += [run(_O_GATE + TN * t) for t in range(4)]
    return np.asarray(tiles, np.int32).reshape(-1)


def _wprep_body(tbl_ref, s0, s1, s2, s3, s4, o_ref):
    del tbl_ref
    j = pl.program_id(1)
    src = (s0, s1, s2, s3, s4)
    lo = lax.broadcasted_iota(jnp.int32, (D_MODEL, LANE), 1) < LANE // 2

    def aligned(s):
        return src[s][...].astype(BF16)

    def shifted(s):
        return jnp.where(lo, pltpu.roll(src[s][...], LANE // 2, 1), pltpu.roll(src[s + 1][...], LANE // 2, 1)).astype(BF16)

    def put(g, val):
        o_ref[:, g * LANE:(g + 1) * LANE] = val

    is_aligned = (j == COL_AQ // TN) | (j == COL_AK // TN) | (j == COL_AV // TN)
    is_mix = j == COL_MIX // TN
    is_mix2 = j == COL_MIX2 // TN

    @pl.when(is_aligned)
    def _():
        for g in range(4):
            put(g, aligned(g))

    @pl.when(jnp.logical_not(is_aligned | is_mix | is_mix2))
    def _():
        for g in range(4):
            put(g, shifted(g))

    @pl.when(is_mix)
    def _():
        put(0, shifted(0))
        put(1, jnp.where(lo, src[2][...], 0.0).astype(BF16))
        put(2, aligned(3))
        put(3, aligned(4))

    @pl.when(is_mix2)
    def _():
        for g in range(3):
            put(g, aligned(g))
        put(3, shifted(3))


def _permute_w_in(w_in):
    def src_spec(s):
        return pl.BlockSpec((None, D_MODEL, LANE), lambda l, j, tbl: (l, 0, tbl[j * N_SRC + s]))

    return pl.pallas_call(
        _wprep_body,
        grid_spec=pltpu.PrefetchScalarGridSpec(
            num_scalar_prefetch=1,
            grid=(DEPTH, N_PROJ // TN),
            in_specs=[src_spec(s) for s in range(N_SRC)],
            out_specs=pl.BlockSpec((None, D_MODEL, TN), lambda l, j, tbl: (l, 0, j))),
        out_shape=jax.ShapeDtypeStruct((DEPTH, D_MODEL, N_PROJ), BF16),
        compiler_params=pltpu.CompilerParams(dimension_semantics=("arbitrary", "arbitrary"),
                                             vmem_limit_bytes=VMEM_LIMIT),
        name="w_in_relayout",
    )(jnp.asarray(_src_blocks()), w_in, w_in, w_in, w_in, w_in)


def _permute_w_uq(w_uq):
    L = w_uq.shape[0]
    w = w_uq.reshape(L, B_Q_RANK, B_HEADS, B_NOPE + B_ROPE)
    w = jnp.pad(w, ((0, 0), (0, 0), (0, 0), (0, 2 * LANE - B_NOPE - B_ROPE)))
    return w.reshape(L, B_Q_RANK, B_HEADS * 2 * LANE).astype(BF16)


def _rope_angles(dim):
    pos = jnp.arange(SEQ, dtype=jnp.int32)
    inv = ROPE_THETA ** (-jnp.arange(0, dim, 2, dtype=jnp.float32) / dim)
    ang = pos.astype(jnp.float32)[:, None] * inv[None, :]
    return jnp.cos(ang), jnp.sin(ang)


def _rope_tables():
    c, s = _rope_angles(HEAD_DIM)
    cos128 = jnp.concatenate([c, c], -1)
    sin128 = jnp.concatenate([-s, s], -1)
    c, s = _rope_angles(D_HEAD_DIM)
    z = jnp.zeros_like(s)
    cos64 = jnp.concatenate([c, c, c, c], -1)
    sin64a = jnp.concatenate([z, s, z, s], -1)
    sin64b = jnp.concatenate([-s, z, -s, z], -1)
    return cos128, sin128, cos64, sin64a, sin64b


TM_IN = 1024


def _rope128(x, c_ref, s_ref):
    return x * c_ref[...] + pltpu.roll(x, 64, 1) * s_ref[...]


def _rope64(x, c_ref, sa_ref, sb_ref):
    return x * c_ref[...] + pltpu.roll(x, 32, 1) * sa_ref[...] + pltpu.roll(x, 96, 1) * sb_ref[...]


def _inproj_body(x_ref, w_ref, c128, s128, c64, sa64, sb64, o_ref):
    j = pl.program_id(1)

    def tile():
        return jnp.dot(x_ref[...], w_ref[...], preferred_element_type=F32)

    groups = [slice(g * LANE, (g + 1) * LANE) for g in range(TN // LANE)]

    @pl.when(j < 4)
    def _():
        scale = jnp.where((j == 0) | (j == 2), HEAD_DIM ** -0.5, 1.0).astype(F32)
        acc = tile()
        for sl in groups:
            o_ref[:, sl] = (_rope128(acc[:, sl], c128, s128) * scale).astype(BF16)

    @pl.when(j == 4)
    def _():
        acc = tile()
        for sl in groups:
            o_ref[:, sl] = (_rope64(acc[:, sl], c64, sa64, sb64) * (D_HEAD_DIM ** -0.5)).astype(BF16)

    @pl.when(j == 5)
    def _():
        acc = tile()
        for sl in groups[:2]:
            o_ref[:, sl] = _rope64(acc[:, sl], c64, sa64, sb64).astype(BF16)
        o_ref[:, 2 * LANE:] = acc[:, 2 * LANE:].astype(BF16)

    @pl.when(j > 5)
    def _():
        o_ref[...] = tile().astype(BF16)


def _inproj(xb, w_all, layer, tables):
    nt_pos = SEQ // TM_IN
    tab_spec = pl.BlockSpec((TM_IN, LANE), lambda i, j: (i % nt_pos, 0))
    return pl.pallas_call(
        _inproj_body,
        grid=(ROWS // TM_IN, N_PROJ // TN),
        in_specs=[pl.BlockSpec((TM_IN, D_MODEL), lambda i, j: (i, 0)),
                  pl.BlockSpec((None, D_MODEL, TN), lambda i, j: (layer, 0, j))] + [tab_spec] * 5,
        out_specs=pl.BlockSpec((TM_IN, TN), lambda i, j: (i, j)),
        out_shape=jax.ShapeDtypeStruct((ROWS, N_PROJ), BF16),
        compiler_params=pltpu.CompilerParams(dimension_semantics=("arbitrary", "arbitrary"),
                                             vmem_limit_bytes=VMEM_LIMIT),
        name="inproj",
    )(xb, w_all, *tables)


def _a_body(q_ref, k_ref, v_ref, o_ref, qf, kf, vf, m_s, l_s, acc_s):
    qf[...] = q_ref[...].astype(F32)
    kf[...] = k_ref[...].astype(F32)
    vf[...] = v_ref[...].astype(F32)
    row = lax.broadcasted_iota(jnp.int32, (BLK, BLK), 0)
    col = lax.broadcasted_iota(jnp.int32, (BLK, BLK), 1)
    own_ok = col <= row
    prev_ok = col >= row
    both_ok = jnp.concatenate([prev_ok, own_ok], axis=1)

    def rows(start, size, d):
        return pl.ds(start, size) if d == 1 else pl.ds(start, size, stride=d)

    def kv_rows(d, start, has_prev):
        return rows(start - d * BLK, 2 * BLK, d) if has_prev else rows(start, BLK, d)

    def scores(p, d, start, has_prev):
        q = qf[rows(start, BLK, d), :].astype(BF16)
        kk = kf[kv_rows(d, start, has_prev), :].astype(BF16)
        return jnp.where(both_ok if has_prev else own_ok, _nt(q, kk), NEG)

    def finish(s, p, d, start, has_prev):
        vv = vf[kv_rows(d, start, has_prev), :].astype(BF16)
        m = jnp.max(s, axis=-1, keepdims=True)
        e = jnp.exp(s - m)
        l = jnp.sum(e, axis=-1, keepdims=True)
        acc = jnp.dot(e.astype(BF16), vv, preferred_element_type=F32)
        dst = rows(start, BLK, d)
        m_s[p, dst, :] = jnp.broadcast_to(m, (BLK, LANE))
        l_s[p, dst, :] = jnp.broadcast_to(l, (BLK, LANE))
        acc_s[p, dst, :] = acc

    blocks = [(p, d, r + d * BLK * n, n > 0)
              for p, (_, d) in enumerate(A_PATTERNS) for r in range(d) for n in range(SEQ // d // BLK)]
    s_next = scores(*blocks[0])
    for i, blk in enumerate(blocks):
        s = s_next
        if i + 1 < len(blocks):
            s_next = scores(*blocks[i + 1])
        finish(s, *blk)

    CH = 256

    def combine(c, carry):
        sl = pl.ds(pl.multiple_of(c * CH, CH), CH)
        ms = [m_s[p, sl, :] for p in range(3)]
        mx = jnp.maximum(jnp.maximum(ms[0], ms[1]), ms[2])
        ws = [jnp.exp(m - mx) for m in ms]
        num = ws[0] * acc_s[0, sl, :] + ws[1] * acc_s[1, sl, :] + ws[2] * acc_s[2, sl, :]
        den = ws[0] * l_s[0, sl, :] + ws[1] * l_s[1, sl, :] + ws[2] * l_s[2, sl, :]
        o_ref[sl, :] = (num / den).astype(BF16)
        return carry

    lax.fori_loop(0, SEQ // CH, combine, 0)


def _a_attn(proj):
    nh = BRANCH_W // HEAD_DIM

    def spec(col):
        return pl.BlockSpec((SEQ, HEAD_DIM), lambda b, h: (b, col // HEAD_DIM + h))

    return pl.pallas_call(
        _a_body,
        grid=(BATCH, nh),
        in_specs=[spec(COL_AQ), spec(COL_AK), spec(COL_AV)],
        out_specs=pl.BlockSpec((SEQ, HEAD_DIM), lambda b, h: (b, h)),
        out_shape=jax.ShapeDtypeStruct((ROWS, BRANCH_W), BF16),
        scratch_shapes=[pltpu.VMEM((SEQ, HEAD_DIM), F32)] * 3 + [pltpu.VMEM((3, SEQ, LANE), F32)] * 3,
        compiler_params=pltpu.CompilerParams(dimension_semantics=("arbitrary", "arbitrary"),
                                             vmem_limit_bytes=VMEM_LIMIT),
        name="mixer_a",
    )(proj, proj, proj)


QB = 256
NQB = SEQ // QB


def _rows(ref, i):
    return ref[i * QB:(i + 1) * QB, :]


def _causal_attend(q_ref, k_ref, vt_s, o_ref, bias_of):
    kk = lax.broadcasted_iota(jnp.int32, (QB, QB), 0)
    qq = lax.broadcasted_iota(jnp.int32, (QB, QB), 1)
    causal = kk <= qq
    order = list(range(NQB - 1, -1, -1))

    def scores(n):
        return _nt(k_ref[0:(n + 1) * QB, :], _rows(q_ref, n))

    s_next = scores(order[0])
    for i, n in enumerate(order):
        s_t = s_next
        if i + 1 < len(order):
            s_next = scores(order[i + 1])
        blocks = [s_t[j * QB:(j + 1) * QB] for j in range(n + 1)]
        blocks[n] = jnp.where(causal, blocks[n], NEG)
        for j in range(n):
            bias = bias_of(n, j)
            if bias is not None:
                blocks[j] = blocks[j] + bias
        m = functools.reduce(jnp.maximum, [jnp.max(b, axis=0, keepdims=True) for b in blocks])
        ps = [jnp.exp(b - m) for b in blocks]
        l = functools.reduce(jnp.add, [jnp.sum(p, axis=0, keepdims=True) for p in ps])
        p_all = jnp.concatenate([p.astype(BF16) for p in ps], axis=0)
        acc_t = jnp.dot(vt_s[:, 0:(n + 1) * QB], p_all, preferred_element_type=F32)
        o_ref[n * QB:(n + 1) * QB, :] = (acc_t * (1.0 / l)).T.astype(BF16)


TM_BP = 512


def _rms(x, g_ref):
    xf = x.astype(F32)
    return xf * lax.rsqrt(jnp.mean(xf * xf, -1, keepdims=True) + RMS_EPS) * g_ref[...]


def _bprep_body(mix_ref, mix2_ref, qn_ref, wuq_ref, kvn_ref, wukv_ref, c64, sa64, sb64, q_ref, k_ref, v_ref):
    cq = _rms(mix2_ref[:, :B_Q_RANK], qn_ref).astype(BF16)
    q = jnp.dot(cq, wuq_ref[...], preferred_element_type=F32)
    scale = (B_NOPE + B_ROPE) ** -0.5
    for h in range(B_HEADS):
        o = h * 2 * LANE
        q_ref[:, o:o + LANE] = (q[:, o:o + LANE] * scale).astype(BF16)
        q_ref[:, o + LANE:o + 2 * LANE] = (_rope64(q[:, o + LANE:o + 2 * LANE], c64, sa64, sb64) * scale).astype(BF16)
    ckv = _rms(mix_ref[:, 2 * LANE:], kvn_ref).astype(BF16)
    kv = jnp.dot(ckv, wukv_ref[...], preferred_element_type=F32)
    kr = mix_ref[:, LANE:2 * LANE]
    for h in range(B_HEADS):
        o = h * 2 * LANE
        k_ref[:, o:o + LANE] = kv[:, o:o + LANE].astype(BF16)
        k_ref[:, o + LANE:o + 2 * LANE] = kr
        v_ref[:, h * LANE:(h + 1) * LANE] = kv[:, o + LANE:o + 2 * LANE].astype(BF16)


def _bprep(proj, q_norm, w_uq_p, kv_norm, w_ukv, tables):
    nt_pos = SEQ // TM_BP
    tab_spec = pl.BlockSpec((TM_BP, LANE), lambda i: (i % nt_pos, 0))
    full = lambda shape: pl.BlockSpec(shape, lambda i: (0, 0))
    return pl.pallas_call(
        _bprep_body,
        grid=(ROWS // TM_BP,),
        in_specs=[pl.BlockSpec((TM_BP, TN), lambda i: (i, COL_MIX // TN)),
                  pl.BlockSpec((TM_BP, TN), lambda i: (i, COL_MIX2 // TN)),
                  full((1, B_Q_RANK)), full((B_Q_RANK, B_HEADS * 2 * LANE)),
                  full((1, B_KV_RANK)), full((B_KV_RANK, B_HEADS * 2 * LANE))] + [tab_spec] * 3,
        out_specs=[pl.BlockSpec((TM_BP, B_HEADS * 2 * LANE), lambda i: (i, 0)),
                   pl.BlockSpec((TM_BP, B_HEADS * 2 * LANE), lambda i: (i, 0)),
                   pl.BlockSpec((TM_BP, B_HEADS * LANE), lambda i: (i, 0))],
        out_shape=[jax.ShapeDtypeStruct((ROWS, B_HEADS * 2 * LANE), BF16),
                   jax.ShapeDtypeStruct((ROWS, B_HEADS * 2 * LANE), BF16),
                   jax.ShapeDtypeStruct((ROWS, B_HEADS * LANE), BF16)],
        compiler_params=pltpu.CompilerParams(dimension_semantics=("arbitrary",),
                                             vmem_limit_bytes=VMEM_LIMIT),
        name="mla_prep",
    )(proj, proj, q_norm, w_uq_p, kv_norm, w_ukv, *tables[2:])


def _store_vt(v_ref, vt_s):
    for j in range(NQB):
        vt_s[:, j * QB:(j + 1) * QB] = _rows(v_ref, j).astype(F32).T.astype(BF16)


def _battn_body(q_ref, k_ref, v_ref, o_ref, vt_s):
    _store_vt(v_ref, vt_s)
    _causal_attend(q_ref, k_ref, vt_s, o_ref, lambda n, j: None)


def _battn(qb, kb, vb):
    return pl.pallas_call(
        _battn_body,
        grid=(BATCH, B_HEADS),
        in_specs=[pl.BlockSpec((SEQ, 2 * LANE), lambda b, h: (b, h)),
                  pl.BlockSpec((SEQ, 2 * LANE), lambda b, h: (b, h)),
                  pl.BlockSpec((SEQ, B_V), lambda b, h: (b, h))],
        out_specs=pl.BlockSpec((SEQ, B_V), lambda b, h: (b, h)),
        out_shape=jax.ShapeDtypeStruct((ROWS, BRANCH_W), BF16),
        scratch_shapes=[pltpu.VMEM((B_V, SEQ), BF16)],
        compiler_params=pltpu.CompilerParams(dimension_semantics=("arbitrary",) * 2,
                                             vmem_limit_bytes=VMEM_LIMIT),
        name="mla_attn",
    )(qb, kb, vb)


KM_ROWS = 16


def _cattn_body(q_ref, k_ref, v_ref, o_ref, vt_s, km_s, bias_s):
    _store_vt(v_ref, vt_s)
    km_s[...] = jnp.zeros_like(km_s)
    for j in range(NQB):
        km_s[j:j + 1, :] = jnp.sum(_rows(k_ref, j).astype(F32), axis=0, keepdims=True) * (1.0 / MOBA_BLOCK)
    km = km_s[...]
    km_hi = km.astype(BF16)
    km_lo = (km - km_hi.astype(F32)).astype(BF16)
    jrow = lax.broadcasted_iota(jnp.int32, (KM_ROWS, QB), 0)
    for n in range(MOBA_TOPK + 1, NQB):
        q = _rows(q_ref, n)
        g_t = _nt(km_hi, q) + _nt(km_lo, q)
        cnt = jnp.zeros((KM_ROWS, QB), F32)
        for jp in range(n):
            gj = g_t[jp:jp + 1, :]
            beats = (gj > g_t) | ((gj == g_t) & (jp < jrow))
            cnt = cnt + jnp.where(beats, 1.0, 0.0)
        bias_s[n] = jnp.where(cnt < MOBA_TOPK, 0.0, NEG)

    def bias_of(n, j):
        return bias_s[n, j:j + 1, :] if n > MOBA_TOPK else None

    _causal_attend(q_ref, k_ref, vt_s, o_ref, bias_of)


def _cattn(proj):
    def spec(col):
        return pl.BlockSpec((SEQ, HEAD_DIM), lambda b, h: (b, col // HEAD_DIM + h))

    return pl.pallas_call(
        _cattn_body,
        grid=(BATCH, BRANCH_W // HEAD_DIM),
        in_specs=[spec(COL_CQ), spec(COL_CK), spec(COL_CV)],
        out_specs=pl.BlockSpec((SEQ, HEAD_DIM), lambda b, h: (b, h)),
        out_shape=jax.ShapeDtypeStruct((ROWS, BRANCH_W), BF16),
        scratch_shapes=[pltpu.VMEM((HEAD_DIM, SEQ), BF16), pltpu.VMEM((KM_ROWS, HEAD_DIM), F32),
                        pltpu.VMEM((NQB, KM_ROWS, QB), F32)],
        compiler_params=pltpu.CompilerParams(dimension_semantics=("arbitrary",) * 2,
                                             vmem_limit_bytes=VMEM_LIMIT),
        name="moba_attn",
    )(proj, proj, proj)


D_REP = D_HEADS // D_KV_HEADS


def _dattn_body(sink_ref, q_ref, k_ref, v_ref, o_ref, kk_s, vv_s):
    lane_full = lax.broadcasted_iota(jnp.int32, (SEQ, LANE), 1) < D_HEAD_DIM
    kf = k_ref[...].astype(F32)
    vf = v_ref[...].astype(F32)
    kr = pltpu.roll(kf, D_HEAD_DIM, 1)
    vr = pltpu.roll(vf, D_HEAD_DIM, 1)
    kk_s[0] = jnp.where(lane_full, kf, kr).astype(BF16)
    kk_s[1] = jnp.where(lane_full, kr, kf).astype(BF16)
    vv_s[0] = jnp.where(lane_full, vf, vr).astype(BF16)
    vv_s[1] = jnp.where(lane_full, vr, vf).astype(BF16)

    lo = lax.broadcasted_iota(jnp.int32, (BLK, LANE), 1) < D_HEAD_DIM
    row = lax.broadcasted_iota(jnp.int32, (BLK, BLK), 0)
    col = lax.broadcasted_iota(jnp.int32, (BLK, BLK), 1)
    own_ok = jnp.concatenate([col <= row] * D_REP, axis=0)
    prev_ok = jnp.concatenate([col > row] * D_REP, axis=0)
    both_ok = jnp.concatenate([prev_ok, own_ok], axis=1)
    hrow = lax.broadcasted_iota(jnp.int32, (D_REP * BLK, 1), 0) // BLK

    sinks = []
    for g in range(D_KV_HEADS):
        sink = jnp.zeros((D_REP * BLK, 1), F32)
        for i in range(D_REP):
            sink = jnp.where(hrow == i, sink_ref[g * D_REP + i], sink)
        sinks.append(sink)

    def kv_rows(n):
        return pl.ds((n - 1) * BLK, 2 * BLK) if n > 0 else pl.ds(0, BLK)

    def scores(g, n):
        zero = jnp.zeros((BLK, LANE), BF16)
        parts = []
        for pg in range(2):
            c0 = (2 * g + pg) * LANE
            qp = q_ref[n * BLK:(n + 1) * BLK, c0:c0 + LANE]
            parts += [jnp.where(lo, qp, zero), jnp.where(lo, zero, qp)]
        q4 = jnp.concatenate(parts, axis=0)
        return jnp.where(both_ok if n > 0 else own_ok, _nt(q4, kk_s[g, kv_rows(n), :]), NEG)

    def finish(s, g, n):
        m2 = jnp.maximum(jnp.max(s, axis=-1, keepdims=True), sinks[g])
        e = jnp.exp(s - m2)
        den = jnp.sum(e, axis=-1, keepdims=True) + jnp.exp(sinks[g] - m2)
        o4 = jnp.dot(e.astype(BF16), vv_s[g, kv_rows(n), :], preferred_element_type=F32) / den
        for pg in range(2):
            c0 = (2 * g + pg) * LANE
            o_ref[n * BLK:(n + 1) * BLK, c0:c0 + LANE] = jnp.where(
                lo, o4[(2 * pg) * BLK:(2 * pg + 1) * BLK], o4[(2 * pg + 1) * BLK:(2 * pg + 2) * BLK]).astype(BF16)

    blocks = [(g, n) for g in range(D_KV_HEADS) for n in range(SEQ // BLK)]
    s_next = scores(*blocks[0])
    for i, blk in enumerate(blocks):
        s = s_next
        if i + 1 < len(blocks):
            s_next = scores(*blocks[i + 1])
        finish(s, *blk)


def _dattn(proj, sinks):
    return pl.pallas_call(
        _dattn_body,
        grid=(BATCH,),
        in_specs=[pl.BlockSpec(memory_space=pltpu.SMEM),
                  pl.BlockSpec((SEQ, BRANCH_W), lambda b: (b, COL_DQ // BRANCH_W)),
                  pl.BlockSpec((SEQ, LANE), lambda b: (b, COL_DK // LANE)),
                  pl.BlockSpec((SEQ, LANE), lambda b: (b, COL_DV // LANE))],
        out_specs=pl.BlockSpec((SEQ, BRANCH_W), lambda b: (b, 0)),
        out_shape=jax.ShapeDtypeStruct((ROWS, BRANCH_W), BF16),
        scratch_shapes=[pltpu.VMEM((D_KV_HEADS, SEQ, LANE), BF16)] * 2,
        compiler_params=pltpu.CompilerParams(dimension_semantics=("arbitrary",),
                                             vmem_limit_bytes=VMEM_LIMIT),
        name="sink_swa",
    )(sinks, proj, proj, proj)


TM_OUT = 512


def _outproj_body(ya, yb, yc, yd, ga, gb, gc, gd, x_ref, w_ref, bn_ref, g_ref, b_ref, of_ref, ob_ref):
    acc = DN_ALPHA * x_ref[...]
    for i, (y_ref, gate_ref) in enumerate(((ya, ga), (yb, gb), (yc, gc), (yd, gd))):
        sl = slice(i * BRANCH_W, (i + 1) * BRANCH_W)
        y = y_ref[...].astype(F32)
        y = y * lax.rsqrt(jnp.mean(y * y, -1, keepdims=True) + RMS_EPS) * bn_ref[:, sl]
        gt = gate_ref[...].astype(F32)
        y = (y * (gt * jax.nn.sigmoid(gt))).astype(BF16)
        acc = acc + jnp.dot(y, w_ref[sl, :], preferred_element_type=F32)
    mu = jnp.mean(acc, -1, keepdims=True)
    xc = acc - mu
    var = jnp.mean(xc * xc, -1, keepdims=True)
    out = xc * lax.rsqrt(var + LN_EPS) * g_ref[...] + b_ref[...]
    of_ref[...] = out
    ob_ref[...] = out.astype(BF16)


def _outproj(ya, yb, yc, yd, proj, xf, w_out_all, layer, bn, ln_g, ln_b):
    y_spec = pl.BlockSpec((TM_OUT, BRANCH_W), lambda i: (i, 0))
    gate_specs = [pl.BlockSpec((TM_OUT, BRANCH_W), functools.partial(lambda i, c: (i, c), c=COL_GATE // BRANCH_W + g))
                  for g in range(4)]
    row_spec = pl.BlockSpec((1, D_MODEL), lambda i: (0, 0))
    x_spec = pl.BlockSpec((TM_OUT, D_MODEL), lambda i: (i, 0))
    return pl.pallas_call(
        _outproj_body,
        grid=(ROWS // TM_OUT,),
        in_specs=[y_spec] * 4 + gate_specs + [x_spec, pl.BlockSpec((None, D_MODEL, D_MODEL), lambda i: (layer, 0, 0)),
                                              row_spec, row_spec, row_spec],
        out_specs=[x_spec, x_spec],
        out_shape=[jax.ShapeDtypeStruct((ROWS, D_MODEL), F32), jax.ShapeDtypeStruct((ROWS, D_MODEL), BF16)],
        compiler_params=pltpu.CompilerParams(dimension_semantics=("arbitrary",),
                                             vmem_limit_bytes=VMEM_LIMIT),
        name="outproj",
    )(ya, yb, yc, yd, proj, proj, proj, proj, xf, w_out_all, bn, ln_g, ln_b)


def kernel(x, w_in, q_norm, w_uq, kv_norm, w_ukv, sinks, branch_norm, w_out, ln_gamma, ln_beta):
    assert x.shape == (BATCH, SEQ, D_MODEL) and w_in.shape == (DEPTH, D_MODEL, sum(SPLIT_SIZES))
    tables = _rope_tables()
    w_in_p = _permute_w_in(w_in)
    w_uq_p = _permute_w_uq(w_uq)
    w_ukv_b = w_ukv.astype(BF16)
    w_out_b = w_out.astype(BF16)
    xf = x.reshape(ROWS, D_MODEL)
    xb = xf.astype(BF16)
    for l in range(DEPTH):
        proj = _inproj(xb, w_in_p, l, tables)
        ya = _a_attn(proj)
        qb, kb, vb = _bprep(proj, q_norm[l][None], w_uq_p[l], kv_norm[l][None], w_ukv_b[l], tables)
        yb = _battn(qb, kb, vb)
        yc = _cattn(proj)
        yd = _dattn(proj, sinks[l])
        xf, xb = _outproj(ya, yb, yc, yd, proj, xf, w_out_b, l, branch_norm[l][None], ln_gamma[l][None], ln_beta[l][None])
    return xf.reshape(BATCH, SEQ, D_MODEL)
```

```python
import functools
import math

import numpy as np
import jax
import jax.numpy as jnp
from jax import lax
from jax.experimental import pallas as pl
from jax.experimental.pallas import tpu as pltpu

F32 = jnp.float32
BF16 = jnp.bfloat16

D_MODEL = 2048
BATCH = 4
SEQ = 2048
DEPTH = 4
BRANCH_W = 512
HEAD_DIM = 128
A_PATTERNS = ((128, 1), (512, 4), (2048, 16))
B_HEADS = 4
B_Q_RANK = 384
B_KV_RANK = 256
B_NOPE = 128
B_ROPE = 64
B_V = 128
MOBA_BLOCK = 256
MOBA_TOPK = 3
D_HEAD_DIM = 64
D_HEADS = 8
D_KV_HEADS = 2
ROPE_THETA = 10000.0
RMS_EPS = 1e-6
LN_EPS = 1e-5
NEG = -1e30
DN_ALPHA = (2 * DEPTH) ** 0.25
SPLIT_SIZES = (512, 512, 512, 384, 256, 64, 512, 512, 512, 512, 128, 128, 2048)

LANE = 128
BLK = 128
ROWS = BATCH * SEQ

TN = 512
COL_AQ, COL_AK, COL_CQ, COL_CK, COL_DQ = 0, 512, 1024, 1536, 2048
COL_MIX = 2560
COL_DK, COL_KR, COL_CKV = 2560, 2688, 2816
COL_AV, COL_CV = 3072, 3584
COL_MIX2 = 4096
COL_BCQ, COL_DV = 4096, 4480
COL_GATE = 4608
N_PROJ = 6656

VMEM_LIMIT = 48 * 1024 * 1024


def _nt(a, b):
    return lax.dot_general(a, b, (((1,), (1,)), ((), ())), preferred_element_type=F32)


_OFFS = np.cumsum((0,) + SPLIT_SIZES)
(_O_AQ, _O_AK, _O_AV, _O_BCQ, _O_BCKV, _O_BKR, _O_CQ, _O_CK, _O_CV, _O_DQ, _O_DK, _O_DV, _O_GATE) = (int(o) for o in _OFFS[:-1])


def _src_rows():
    def run(start):
        return [start + LANE * g for g in range(TN // LANE)]

    tiles = [run(_O_AQ), run(_O_AK), run(_O_CQ), run(_O_CK), run(_O_DQ),
             [_O_DK, _O_BKR, _O_BCKV, _O_BCKV + LANE],
             run(_O_AV), run(_O_CV),
             [_O_BCQ, _O_BCQ + LANE, _O_BCQ + 2 * LANE, _O_DV]]
    tiles += [run(_O_GATE + TN * t) for t in range(4)]
    return np.asarray(tiles, np.int32).reshape(-1)


def _permute_w_uq(w_uq):
    L = w_uq.shape[0]
    w = w_uq.reshape(L, B_Q_RANK, B_HEADS, B_NOPE + B_ROPE)
    w = jnp.pad(w, ((0, 0), (0, 0), (0, 0), (0, 2 * LANE - B_NOPE - B_ROPE)))
    return w.reshape(L, B_Q_RANK, B_HEADS * 2 * LANE).astype(BF16)


def _rope_angles(dim):
    pos = jnp.arange(SEQ, dtype=jnp.int32)
    inv = ROPE_THETA ** (-jnp.arange(0, dim, 2, dtype=jnp.float32) / dim)
    ang = pos.astype(jnp.float32)[:, None] * inv[None, :]
    return jnp.cos(ang), jnp.sin(ang)


def _rope_tables():
    c, s = _rope_angles(HEAD_DIM)
    cos128 = jnp.concatenate([c, c], -1)
    sin128 = jnp.concatenate([-s, s], -1)
    c, s = _rope_angles(D_HEAD_DIM)
    z = jnp.zeros_like(s)
    cos64 = jnp.concatenate([c, c, c, c], -1)
    sin64a = jnp.concatenate([z, s, z, s], -1)
    sin64b = jnp.concatenate([-s, z, -s, z], -1)
    return cos128, sin128, cos64, sin64a, sin64b


TM_IN = 1024


def _rope128(x, c, s):
    return x * c + pltpu.roll(x, 64, 1) * s


def _rope64(x, c, sa, sb):
    return x * c + pltpu.roll(x, 32, 1) * sa + pltpu.roll(x, 96, 1) * sb


N_GRP = TN // LANE


def _inproj_body(tbl_ref, x_ref, w0, w1, w2, w3, c128, s128, c64, sa64, sb64, o_ref, wt_s):
    del tbl_ref
    j = pl.program_id(0)
    i = pl.program_id(1)

    @pl.when(i == 0)
    def _():
        for g, w in enumerate((w0, w1, w2, w3)):
            wt_s[g * LANE:(g + 1) * LANE, :] = w[0].astype(BF16)

        @pl.when(j == COL_MIX // TN)
        def _():
            wt_s[LANE + B_ROPE:2 * LANE, :] = jnp.zeros((LANE - B_ROPE, D_MODEL), BF16)

    pos = pl.ds(pl.multiple_of((i % (SEQ // TM_IN)) * TM_IN, TM_IN), TM_IN)

    def tile():
        return _nt(x_ref[...], wt_s[...])

    groups = [slice(g * LANE, (g + 1) * LANE) for g in range(N_GRP)]

    @pl.when(j < 4)
    def _():
        scale = jnp.where((j == 0) | (j == 2), HEAD_DIM ** -0.5, 1.0).astype(F32)
        acc = tile()
        c, s = c128[pos, :], s128[pos, :]
        for sl in groups:
            o_ref[:, sl] = (_rope128(acc[:, sl], c, s) * scale).astype(BF16)

    @pl.when(j == 4)
    def _():
        acc = tile()
        c, sa, sb = c64[pos, :], sa64[pos, :], sb64[pos, :]
        for sl in groups:
            o_ref[:, sl] = (_rope64(acc[:, sl], c, sa, sb) * (D_HEAD_DIM ** -0.5)).astype(BF16)

    @pl.when(j == 5)
    def _():
        acc = tile()
        c, sa, sb = c64[pos, :], sa64[pos, :], sb64[pos, :]
        for sl in groups[:2]:
            o_ref[:, sl] = _rope64(acc[:, sl], c, sa, sb).astype(BF16)
        o_ref[:, 2 * LANE:] = acc[:, 2 * LANE:].astype(BF16)

    @pl.when(j > 5)
    def _():
        o_ref[...] = tile().astype(BF16)


def _inproj(xb, w_t, layer, tables):
    def w_spec(g):
        return pl.BlockSpec((pl.Element(1), pl.Element(LANE), pl.Element(D_MODEL)),
                            lambda j, i, tbl: (layer, pl.multiple_of(tbl[j * N_GRP + g], B_ROPE), 0))

    tab_spec = pl.BlockSpec((SEQ, LANE), lambda j, i, tbl: (0, 0))
    return pl.pallas_call(
        _inproj_body,
        grid_spec=pltpu.PrefetchScalarGridSpec(
            num_scalar_prefetch=1,
            grid=(N_PROJ // TN, ROWS // TM_IN),
            in_specs=[pl.BlockSpec((TM_IN, D_MODEL), lambda j, i, tbl: (i, 0))]
            + [w_spec(g) for g in range(N_GRP)] + [tab_spec] * 5,
            out_specs=pl.BlockSpec((TM_IN, TN), lambda j, i, tbl: (i, j)),
            scratch_shapes=[pltpu.VMEM((TN, D_MODEL), BF16)]),
        out_shape=jax.ShapeDtypeStruct((ROWS, N_PROJ), BF16),
        compiler_params=pltpu.CompilerParams(dimension_semantics=("arbitrary", "arbitrary"),
                                             vmem_limit_bytes=VMEM_LIMIT),
        name="inproj",
    )(jnp.asarray(_src_rows()), xb, w_t, w_t, w_t, w_t, *tables)


def _a_body(q_ref, k_ref, v_ref, o_ref, qf, kf, vf, m_s, l_s, acc_s):
    qf[...] = q_ref[...].astype(F32)
    kf[...] = k_ref[...].astype(F32)
    vf[...] = v_ref[...].astype(F32)
    row = lax.broadcasted_iota(jnp.int32, (BLK, BLK), 0)
    col = lax.broadcasted_iota(jnp.int32, (BLK, BLK), 1)
    own_ok = col <= row
    prev_ok = col >= row
    both_ok = jnp.concatenate([prev_ok, own_ok], axis=1)

    def rows(start, size, d):
        return pl.ds(start, size) if d == 1 else pl.ds(start, size, stride=d)

    def kv_rows(d, start, has_prev):
        return rows(start - d * BLK, 2 * BLK, d) if has_prev else rows(start, BLK, d)

    def scores(p, d, start, has_prev):
        q = qf[rows(start, BLK, d), :].astype(BF16)
        kk = kf[kv_rows(d, start, has_prev), :].astype(BF16)
        return jnp.where(both_ok if has_prev else own_ok, _nt(q, kk), NEG)

    def finish(s, p, d, start, has_prev):
        vv = vf[kv_rows(d, start, has_prev), :].astype(BF16)
        m = jnp.max(s, axis=-1, keepdims=True)
        e = jnp.exp(s - m)
        l = jnp.sum(e, axis=-1, keepdims=True)
        acc = jnp.dot(e.astype(BF16), vv, preferred_element_type=F32)
        dst = rows(start, BLK, d)
        m_s[p, dst, :] = jnp.broadcast_to(m, (BLK, LANE))
        l_s[p, dst, :] = jnp.broadcast_to(l, (BLK, LANE))
        acc_s[p, dst, :] = acc

    blocks = [(p, d, r + d * BLK * n, n > 0)
              for p, (_, d) in enumerate(A_PATTERNS) for r in range(d) for n in range(SEQ // d // BLK)]
    s_next = scores(*blocks[0])
    for i, blk in enumerate(blocks):
        s = s_next
        if i + 1 < len(blocks):
            s_next = scores(*blocks[i + 1])
        finish(s, *blk)

    CH = 256

    def combine(c, carry):
        sl = pl.ds(pl.multiple_of(c * CH, CH), CH)
        ms = [m_s[p, sl, :] for p in range(3)]
        mx = jnp.maximum(jnp.maximum(ms[0], ms[1]), ms[2])
        ws = [jnp.exp(m - mx) for m in ms]
        num = ws[0] * acc_s[0, sl, :] + ws[1] * acc_s[1, sl, :] + ws[2] * acc_s[2, sl, :]
        den = ws[0] * l_s[0, sl, :] + ws[1] * l_s[1, sl, :] + ws[2] * l_s[2, sl, :]
        o_ref[sl, :] = (num / den).astype(BF16)
        return carry

    lax.fori_loop(0, SEQ // CH, combine, 0)


def _a_attn(proj):
    nh = BRANCH_W // HEAD_DIM

    def spec(col):
        return pl.BlockSpec((SEQ, HEAD_DIM), lambda b, h: (b, col // HEAD_DIM + h))

    return pl.pallas_call(
        _a_body,
        grid=(BATCH, nh),
        in_specs=[spec(COL_AQ), spec(COL_AK), spec(COL_AV)],
        out_specs=pl.BlockSpec((SEQ, HEAD_DIM), lambda b, h: (b, h)),
        out_shape=jax.ShapeDtypeStruct((ROWS, BRANCH_W), BF16),
        scratch_shapes=[pltpu.VMEM((SEQ, HEAD_DIM), F32)] * 3 + [pltpu.VMEM((3, SEQ, LANE), F32)] * 3,
        compiler_params=pltpu.CompilerParams(dimension_semantics=("arbitrary", "arbitrary"),
                                             vmem_limit_bytes=VMEM_LIMIT),
        name="mixer_a",
    )(proj, proj, proj)


QB = 256
NQB = SEQ // QB


def _rows(ref, i):
    return ref[i * QB:(i + 1) * QB, :]


def _causal_attend(q_ref, k_ref, vt_s, o_ref, bias_of):
    kk = lax.broadcasted_iota(jnp.int32, (QB, QB), 0)
    qq = lax.broadcasted_iota(jnp.int32, (QB, QB), 1)
    causal = kk <= qq
    order = list(range(NQB - 1, -1, -1))

    def scores(n):
        return _nt(k_ref[0:(n + 1) * QB, :], _rows(q_ref, n))

    s_next = scores(order[0])
    for i, n in enumerate(order):
        s_t = s_next
        if i + 1 < len(order):
            s_next = scores(order[i + 1])
        blocks = [s_t[j * QB:(j + 1) * QB] for j in range(n + 1)]
        blocks[n] = jnp.where(causal, blocks[n], NEG)
        for j in range(n):
            bias = bias_of(n, j)
            if bias is not None:
                blocks[j] = blocks[j] + bias
        m = functools.reduce(jnp.maximum, [jnp.max(b, axis=0, keepdims=True) for b in blocks])
        ps = [jnp.exp(b - m) for b in blocks]
        l = functools.reduce(jnp.add, [jnp.sum(p, axis=0, keepdims=True) for p in ps])
        p_all = jnp.concatenate([p.astype(BF16) for p in ps], axis=0)
        acc_t = jnp.dot(vt_s[:, 0:(n + 1) * QB], p_all, preferred_element_type=F32)
        o_ref[n * QB:(n + 1) * QB, :] = (acc_t * (1.0 / l)).T.astype(BF16)


TM_BP = 512


def _rms(x, g_ref):
    xf = x.astype(F32)
    return xf * lax.rsqrt(jnp.mean(xf * xf, -1, keepdims=True) + RMS_EPS) * g_ref[...]


def _bprep_body(mix_ref, mix2_ref, qn_ref, wuq_ref, kvn_ref, wukv_ref, c64, sa64, sb64, q_ref, k_ref, v_ref):
    cq = _rms(mix2_ref[:, :B_Q_RANK], qn_ref).astype(BF16)
    q = jnp.dot(cq, wuq_ref[...], preferred_element_type=F32)
    scale = (B_NOPE + B_ROPE) ** -0.5
    for h in range(B_HEADS):
        o = h * 2 * LANE
        q_ref[:, o:o + LANE] = (q[:, o:o + LANE] * scale).astype(BF16)
        q_ref[:, o + LANE:o + 2 * LANE] = (_rope64(q[:, o + LANE:o + 2 * LANE], c64[...], sa64[...], sb64[...]) * scale).astype(BF16)
    ckv = _rms(mix_ref[:, 2 * LANE:], kvn_ref).astype(BF16)
    kv = jnp.dot(ckv, wukv_ref[...], preferred_element_type=F32)
    kr = mix_ref[:, LANE:2 * LANE]
    for h in range(B_HEADS):
        o = h * 2 * LANE
        k_ref[:, o:o + LANE] = kv[:, o:o + LANE].astype(BF16)
        k_ref[:, o + LANE:o + 2 * LANE] = kr
        v_ref[:, h * LANE:(h + 1) * LANE] = kv[:, o + LANE:o + 2 * LANE].astype(BF16)


def _bprep(proj, q_norm, w_uq_p, kv_norm, w_ukv, tables):
    nt_pos = SEQ // TM_BP
    tab_spec = pl.BlockSpec((TM_BP, LANE), lambda i: (i % nt_pos, 0))
    full = lambda shape: pl.BlockSpec(shape, lambda i: (0, 0))
    return pl.pallas_call(
        _bprep_body,
        grid=(ROWS // TM_BP,),
        in_specs=[pl.BlockSpec((TM_BP, TN), lambda i: (i, COL_MIX // TN)),
                  pl.BlockSpec((TM_BP, TN), lambda i: (i, COL_MIX2 // TN)),
                  full((1, B_Q_RANK)), full((B_Q_RANK, B_HEADS * 2 * LANE)),
                  full((1, B_KV_RANK)), full((B_KV_RANK, B_HEADS * 2 * LANE))] + [tab_spec] * 3,
        out_specs=[pl.BlockSpec((TM_BP, B_HEADS * 2 * LANE), lambda i: (i, 0)),
                   pl.BlockSpec((TM_BP, B_HEADS * 2 * LANE), lambda i: (i, 0)),
                   pl.BlockSpec((TM_BP, B_HEADS * LANE), lambda i: (i, 0))],
        out_shape=[jax.ShapeDtypeStruct((ROWS, B_HEADS * 2 * LANE), BF16),
                   jax.ShapeDtypeStruct((ROWS, B_HEADS * 2 * LANE), BF16),
                   jax.ShapeDtypeStruct((ROWS, B_HEADS * LANE), BF16)],
        compiler_params=pltpu.CompilerParams(dimension_semantics=("arbitrary",),
                                             vmem_limit_bytes=VMEM_LIMIT),
        name="mla_prep",
    )(proj, proj, q_norm, w_uq_p, kv_norm, w_ukv, *tables[2:])


def _store_vt(v_ref, vt_s):
    for j in range(NQB):
        vt_s[:, j * QB:(j + 1) * QB] = _rows(v_ref, j).astype(F32).T.astype(BF16)


def _battn_body(q_ref, k_ref, v_ref, o_ref, vt_s):
    _store_vt(v_ref, vt_s)
    _causal_attend(q_ref, k_ref, vt_s, o_ref, lambda n, j: None)


def _battn(qb, kb, vb):
    return pl.pallas_call(
        _battn_body,
        grid=(BATCH, B_HEADS),
        in_specs=[pl.BlockSpec((SEQ, 2 * LANE), lambda b, h: (b, h)),
                  pl.BlockSpec((SEQ, 2 * LANE), lambda b, h: (b, h)),
                  pl.BlockSpec((SEQ, B_V), lambda b, h: (b, h))],
        out_specs=pl.BlockSpec((SEQ, B_V), lambda b, h: (b, h)),
        out_shape=jax.ShapeDtypeStruct((ROWS, BRANCH_W), BF16),
        scratch_shapes=[pltpu.VMEM((B_V, SEQ), BF16)],
        compiler_params=pltpu.CompilerParams(dimension_semantics=("arbitrary",) * 2,
                                             vmem_limit_bytes=VMEM_LIMIT),
        name="mla_attn",
    )(qb, kb, vb)


KM_ROWS = 16


def _cattn_body(q_ref, k_ref, v_ref, o_ref, vt_s, km_s, bias_s):
    _store_vt(v_ref, vt_s)
    km_s[...] = jnp.zeros_like(km_s)
    for j in range(NQB):
        km_s[j:j + 1, :] = jnp.sum(_rows(k_ref, j).astype(F32), axis=0, keepdims=True) * (1.0 / MOBA_BLOCK)
    km = km_s[...]
    km_hi = km.astype(BF16)
    km_lo = (km - km_hi.astype(F32)).astype(BF16)
    jrow = lax.broadcasted_iota(jnp.int32, (KM_ROWS, QB), 0)
    for n in range(MOBA_TOPK + 1, NQB):
        q = _rows(q_ref, n)
        g_t = _nt(km_hi, q) + _nt(km_lo, q)
        cnt = jnp.zeros((KM_ROWS, QB), F32)
        for jp in range(n):
            gj = g_t[jp:jp + 1, :]
            beats = (gj > g_t) | ((gj == g_t) & (jp < jrow))
            cnt = cnt + jnp.where(beats, 1.0, 0.0)
        bias_s[n] = jnp.where(cnt < MOBA_TOPK, 0.0, NEG)

    def bias_of(n, j):
        return bias_s[n, j:j + 1, :] if n > MOBA_TOPK else None

    _causal_attend(q_ref, k_ref, vt_s, o_ref, bias_of)


def _cattn(proj):
    def spec(col):
        return pl.BlockSpec((SEQ, HEAD_DIM), lambda b, h: (b, col // HEAD_DIM + h))

    return pl.pallas_call(
        _cattn_body,
        grid=(BATCH, BRANCH_W // HEAD_DIM),
        in_specs=[spec(COL_CQ), spec(COL_CK), spec(COL_CV)],
        out_specs=pl.BlockSpec((SEQ, HEAD_DIM), lambda b, h: (b, h)),
        out_shape=jax.ShapeDtypeStruct((ROWS, BRANCH_W), BF16),
        scratch_shapes=[pltpu.VMEM((HEAD_DIM, SEQ), BF16), pltpu.VMEM((KM_ROWS, HEAD_DIM), F32),
                        pltpu.VMEM((NQB, KM_ROWS, QB), F32)],
        compiler_params=pltpu.CompilerParams(dimension_semantics=("arbitrary",) * 2,
                                             vmem_limit_bytes=VMEM_LIMIT),
        name="moba_attn",
    )(proj, proj, proj)


D_REP = D_HEADS // D_KV_HEADS


def _dattn_body(sink_ref, q_ref, k_ref, v_ref, o_ref, kk_s, vv_s):
    lane_full = lax.broadcasted_iota(jnp.int32, (SEQ, LANE), 1) < D_HEAD_DIM
    kf = k_ref[...].astype(F32)
    vf = v_ref[...].astype(F32)
    kr = pltpu.roll(kf, D_HEAD_DIM, 1)
    vr = pltpu.roll(vf, D_HEAD_DIM, 1)
    kk_s[0] = jnp.where(lane_full, kf, kr).astype(BF16)
    kk_s[1] = jnp.where(lane_full, kr, kf).astype(BF16)
    vv_s[0] = jnp.where(lane_full, vf, vr).astype(BF16)
    vv_s[1] = jnp.where(lane_full, vr, vf).astype(BF16)

    lo = lax.broadcasted_iota(jnp.int32, (BLK, LANE), 1) < D_HEAD_DIM
    row = lax.broadcasted_iota(jnp.int32, (BLK, BLK), 0)
    col = lax.broadcasted_iota(jnp.int32, (BLK, BLK), 1)
    own_ok = jnp.concatenate([col <= row] * D_REP, axis=0)
    prev_ok = jnp.concatenate([col > row] * D_REP, axis=0)
    both_ok = jnp.concatenate([prev_ok, own_ok], axis=1)
    hrow = lax.broadcasted_iota(jnp.int32, (D_REP * BLK, 1), 0) // BLK

    sinks = []
    for g in range(D_KV_HEADS):
        sink = jnp.zeros((D_REP * BLK, 1), F32)
        for i in range(D_REP):
            sink = jnp.where(hrow == i, sink_ref[g * D_REP + i], sink)
        sinks.append(sink)

    def kv_rows(n):
        return pl.ds((n - 1) * BLK, 2 * BLK) if n > 0 else pl.ds(0, BLK)

    def scores(g, n):
        zero = jnp.zeros((BLK, LANE), BF16)
        parts = []
        for pg in range(2):
            c0 = (2 * g + pg) * LANE
            qp = q_ref[n * BLK:(n + 1) * BLK, c0:c0 + LANE]
            parts += [jnp.where(lo, qp, zero), jnp.where(lo, zero, qp)]
        q4 = jnp.concatenate(parts, axis=0)
        return jnp.where(both_ok if n > 0 else own_ok, _nt(q4, kk_s[g, kv_rows(n), :]), NEG)

    def finish(s, g, n):
        m2 = jnp.maximum(jnp.max(s, axis=-1, keepdims=True), sinks[g])
        e = jnp.exp(s - m2)
        den = jnp.sum(e, axis=-1, keepdims=True) + jnp.exp(sinks[g] - m2)
        o4 = jnp.dot(e.astype(BF16), vv_s[g, kv_rows(n), :], preferred_element_type=F32) / den
        for pg in range(2):
            c0 = (2 * g + pg) * LANE
            o_ref[n * BLK:(n + 1) * BLK, c0:c0 + LANE] = jnp.where(
                lo, o4[(2 * pg) * BLK:(2 * pg + 1) * BLK], o4[(2 * pg + 1) * BLK:(2 * pg + 2) * BLK]).astype(BF16)

    blocks = [(g, n) for g in range(D_KV_HEADS) for n in range(SEQ // BLK)]
    s_next = scores(*blocks[0])
    for i, blk in enumerate(blocks):
        s = s_next
        if i + 1 < len(blocks):
            s_next = scores(*blocks[i + 1])
        finish(s, *blk)


def _dattn(proj, sinks):
    return pl.pallas_call(
        _dattn_body,
        grid=(BATCH,),
        in_specs=[pl.BlockSpec(memory_space=pltpu.SMEM),
                  pl.BlockSpec((SEQ, BRANCH_W), lambda b: (b, COL_DQ // BRANCH_W)),
                  pl.BlockSpec((SEQ, LANE), lambda b: (b, COL_DK // LANE)),
                  pl.BlockSpec((SEQ, LANE), lambda b: (b, COL_DV // LANE))],
        out_specs=pl.BlockSpec((SEQ, BRANCH_W), lambda b: (b, 0)),
        out_shape=jax.ShapeDtypeStruct((ROWS, BRANCH_W), BF16),
        scratch_shapes=[pltpu.VMEM((D_KV_HEADS, SEQ, LANE), BF16)] * 2,
        compiler_params=pltpu.CompilerParams(dimension_semantics=("arbitrary",),
                                             vmem_limit_bytes=VMEM_LIMIT),
        name="sink_swa",
    )(sinks, proj, proj, proj)


TM_OUT = 512


def _outproj_body(ya, yb, yc, yd, ga, gb, gc, gd, x_ref, w_ref, bn_ref, g_ref, b_ref, of_ref, ob_ref):
    acc = DN_ALPHA * x_ref[...]
    for i, (y_ref, gate_ref) in enumerate(((ya, ga), (yb, gb), (yc, gc), (yd, gd))):
        sl = slice(i * BRANCH_W, (i + 1) * BRANCH_W)
        y = y_ref[...].astype(F32)
        y = y * lax.rsqrt(jnp.mean(y * y, -1, keepdims=True) + RMS_EPS) * bn_ref[:, sl]
        gt = gate_ref[...].astype(F32)
        y = (y * (gt * jax.nn.sigmoid(gt))).astype(BF16)
        acc = acc + jnp.dot(y, w_ref[sl, :], preferred_element_type=F32)
    mu = jnp.mean(acc, -1, keepdims=True)
    xc = acc - mu
    var = jnp.mean(xc * xc, -1, keepdims=True)
    out = xc * lax.rsqrt(var + LN_EPS) * g_ref[...] + b_ref[...]
    of_ref[...] = out
    ob_ref[...] = out.astype(BF16)


def _outproj(ya, yb, yc, yd, proj, xf, w_out_all, layer, bn, ln_g, ln_b):
    y_spec = pl.BlockSpec((TM_OUT, BRANCH_W), lambda i: (i, 0))
    gate_specs = [pl.BlockSpec((TM_OUT, BRANCH_W), functools.partial(lambda i, c: (i, c), c=COL_GATE // BRANCH_W + g))
                  for g in range(4)]
    row_spec = pl.BlockSpec((1, D_MODEL), lambda i: (0, 0))
    x_spec = pl.BlockSpec((TM_OUT, D_MODEL), lambda i: (i, 0))
    return pl.pallas_call(
        _outproj_body,
        grid=(ROWS // TM_OUT,),
        in_specs=[y_spec] * 4 + gate_specs + [x_spec, pl.BlockSpec((None, D_MODEL, D_MODEL), lambda i: (layer, 0, 0)),
                                              row_spec, row_spec, row_spec],
        out_specs=[x_spec, x_spec],
        out_shape=[jax.ShapeDtypeStruct((ROWS, D_MODEL), F32), jax.ShapeDtypeStruct((ROWS, D_MODEL), BF16)],
        compiler_params=pltpu.CompilerParams(dimension_semantics=("arbitrary",),
                                             vmem_limit_bytes=VMEM_LIMIT),
        name="outproj",
    )(ya, yb, yc, yd, proj, proj, proj, proj, xf, w_out_all, bn, ln_g, ln_b)


def kernel(x, w_in, q_norm, w_uq, kv_norm, w_ukv, sinks, branch_norm, w_out, ln_gamma, ln_beta):
    assert x.shape == (BATCH, SEQ, D_MODEL) and w_in.shape == (DEPTH, D_MODEL, sum(SPLIT_SIZES))
    tables = _rope_tables()
    w_t = jnp.swapaxes(w_in, 1, 2)
    w_uq_p = _permute_w_uq(w_uq)
    w_ukv_b = w_ukv.astype(BF16)
    w_out_b = w_out.astype(BF16)
    xf = x.reshape(ROWS, D_MODEL)
    xb = xf.astype(BF16)
    for l in range(DEPTH):
        proj = _inproj(xb, w_t, l, tables)
        ya = _a_attn(proj)
        qb, kb, vb = _bprep(proj, q_norm[l][None], w_uq_p[l], kv_norm[l][None], w_ukv_b[l], tables)
        yb = _battn(qb, kb, vb)
        yc = _cattn(proj)
        yd = _dattn(proj, sinks[l])
        xf, xb = _outproj(ya, yb, yc, yd, proj, xf, w_out_b, l, branch_norm[l][None], ln_gamma[l][None], ln_beta[l][None])
    return xf.reshape(BATCH, SEQ, D_MODEL)
```

```python
import functools
import math

import numpy as np
import jax
import jax.numpy as jnp
from jax import lax
from jax.experimental import pallas as pl
from jax.experimental.pallas import tpu as pltpu

F32 = jnp.float32
BF16 = jnp.bfloat16

D_MODEL = 2048
BATCH = 4
SEQ = 2048
DEPTH = 4
BRANCH_W = 512
HEAD_DIM = 128
A_PATTERNS = ((128, 1), (512, 4), (2048, 16))
B_HEADS = 4
B_Q_RANK = 384
B_KV_RANK = 256
B_NOPE = 128
B_ROPE = 64
B_V = 128
MOBA_BLOCK = 256
MOBA_TOPK = 3
D_HEAD_DIM = 64
D_HEADS = 8
D_KV_HEADS = 2
ROPE_THETA = 10000.0
RMS_EPS = 1e-6
LN_EPS = 1e-5
NEG = -1e30
DN_ALPHA = (2 * DEPTH) ** 0.25
SPLIT_SIZES = (512, 512, 512, 384, 256, 64, 512, 512, 512, 512, 128, 128, 2048)

LANE = 128
BLK = 128
ROWS = BATCH * SEQ

TN = 512
COL_AQ, COL_AK, COL_CQ, COL_CK, COL_DQ = 0, 512, 1024, 1536, 2048
COL_MIX = 2560
COL_DK, COL_KR, COL_CKV = 2560, 2688, 2816
COL_AV, COL_CV = 3072, 3584
COL_MIX2 = 4096
COL_BCQ, COL_DV = 4096, 4480
COL_GATE = 4608
N_PROJ = 6656

VMEM_LIMIT = 48 * 1024 * 1024


def _nt(a, b):
    return lax.dot_general(a, b, (((1,), (1,)), ((), ())), preferred_element_type=F32)


_OFFS = np.cumsum((0,) + SPLIT_SIZES)
(_O_AQ, _O_AK, _O_AV, _O_BCQ, _O_BCKV, _O_BKR, _O_CQ, _O_CK, _O_CV, _O_DQ, _O_DK, _O_DV, _O_GATE) = (int(o) for o in _OFFS[:-1])


def _src_rows():
    def run(start):
        return [start + LANE * g for g in range(TN // LANE)]

    tiles = [run(_O_AQ), run(_O_AK), run(_O_CQ), run(_O_CK), run(_O_DQ),
             [_O_DK, _O_BKR, _O_BCKV, _O_BCKV + LANE],
             run(_O_AV), run(_O_CV),
             [_O_BCQ, _O_BCQ + LANE, _O_BCQ + 2 * LANE, _O_DV]]
    tiles += [run(_O_GATE + TN * t) for t in range(4)]
    return np.asarray(tiles, np.int32).reshape(-1)


def _permute_w_uq(w_uq):
    L = w_uq.shape[0]
    w = w_uq.reshape(L, B_Q_RANK, B_HEADS, B_NOPE + B_ROPE)
    w = jnp.pad(w, ((0, 0), (0, 0), (0, 0), (0, 2 * LANE - B_NOPE - B_ROPE)))
    return w.reshape(L, B_Q_RANK, B_HEADS * 2 * LANE).astype(BF16)


def _rope_angles(dim):
    pos = jnp.arange(SEQ, dtype=jnp.int32)
    inv = ROPE_THETA ** (-jnp.arange(0, dim, 2, dtype=jnp.float32) / dim)
    ang = pos.astype(jnp.float32)[:, None] * inv[None, :]
    return jnp.cos(ang), jnp.sin(ang)


def _rope_tables():
    c, s = _rope_angles(HEAD_DIM)
    cos128 = jnp.concatenate([c, c], -1)
    sin128 = jnp.concatenate([-s, s], -1)
    c, s = _rope_angles(D_HEAD_DIM)
    z = jnp.zeros_like(s)
    cos64 = jnp.concatenate([c, c, c, c], -1)
    sin64a = jnp.concatenate([z, s, z, s], -1)
    sin64b = jnp.concatenate([-s, z, -s, z], -1)
    return cos128, sin128, cos64, sin64a, sin64b


TM_IN = 1024


def _rope128(x, c, s):
    return x * c + pltpu.roll(x, 64, 1) * s


def _rope64(x, c, sa, sb):
    return x * c + pltpu.roll(x, 32, 1) * sa + pltpu.roll(x, 96, 1) * sb


N_GRP = TN // LANE


def _inproj_body(tbl_ref, x_ref, w0, w1, w2, w3, c128, s128, c64, sa64, sb64, o_ref, wt_s):
    del tbl_ref
    j = pl.program_id(0)
    i = pl.program_id(1)

    @pl.when(i == 0)
    def _():
        for g, w in enumerate((w0, w1, w2, w3)):
            wt_s[g * LANE:(g + 1) * LANE, :] = w[0].astype(BF16)

        @pl.when(j == COL_MIX // TN)
        def _():
            wt_s[LANE + B_ROPE:2 * LANE, :] = jnp.zeros((LANE - B_ROPE, D_MODEL), BF16)

    pos = pl.ds(pl.multiple_of((i % (SEQ // TM_IN)) * TM_IN, TM_IN), TM_IN)

    def tile():
        return _nt(x_ref[...], wt_s[...])

    groups = [slice(g * LANE, (g + 1) * LANE) for g in range(N_GRP)]

    @pl.when(j < 4)
    def _():
        scale = jnp.where((j == 0) | (j == 2), HEAD_DIM ** -0.5, 1.0).astype(F32)
        acc = tile()
        c, s = c128[pos, :], s128[pos, :]
        for sl in groups:
            o_ref[:, sl] = (_rope128(acc[:, sl], c, s) * scale).astype(BF16)

    @pl.when(j == 4)
    def _():
        acc = tile()
        c, sa, sb = c64[pos, :], sa64[pos, :], sb64[pos, :]
        for sl in groups:
            o_ref[:, sl] = (_rope64(acc[:, sl], c, sa, sb) * (D_HEAD_DIM ** -0.5)).astype(BF16)

    @pl.when(j == 5)
    def _():
        acc = tile()
        c, sa, sb = c64[pos, :], sa64[pos, :], sb64[pos, :]
        for sl in groups[:2]:
            o_ref[:, sl] = _rope64(acc[:, sl], c, sa, sb).astype(BF16)
        o_ref[:, 2 * LANE:] = acc[:, 2 * LANE:].astype(BF16)

    @pl.when(j > 5)
    def _():
        o_ref[...] = tile().astype(BF16)


def _inproj(xb, w_t, layer, tables):
    def w_spec(g):
        return pl.BlockSpec((pl.Element(1), pl.Element(LANE), pl.Element(D_MODEL)),
                            lambda j, i, tbl: (layer, pl.multiple_of(tbl[j * N_GRP + g], B_ROPE), 0))

    tab_spec = pl.BlockSpec((SEQ, LANE), lambda j, i, tbl: (0, 0))
    return pl.pallas_call(
        _inproj_body,
        grid_spec=pltpu.PrefetchScalarGridSpec(
            num_scalar_prefetch=1,
            grid=(N_PROJ // TN, ROWS // TM_IN),
            in_specs=[pl.BlockSpec((TM_IN, D_MODEL), lambda j, i, tbl: (i, 0))]
            + [w_spec(g) for g in range(N_GRP)] + [tab_spec] * 5,
            out_specs=pl.BlockSpec((TM_IN, TN), lambda j, i, tbl: (i, j)),
            scratch_shapes=[pltpu.VMEM((TN, D_MODEL), BF16)]),
        out_shape=jax.ShapeDtypeStruct((ROWS, N_PROJ), BF16),
        compiler_params=pltpu.CompilerParams(dimension_semantics=("arbitrary", "arbitrary"),
                                             vmem_limit_bytes=VMEM_LIMIT),
        name="inproj",
    )(jnp.asarray(_src_rows()), xb, w_t, w_t, w_t, w_t, *tables)


A_GROUP = 4
def _a_body(q_ref, k_ref, v_ref, o_ref, qf, kf, vf, m_s, l_s, acc_s):
    qf[...] = q_ref[...].astype(F32)
    kf[...] = k_ref[...].astype(F32)
    vf[...] = v_ref[...].astype(F32)
    row = lax.broadcasted_iota(jnp.int32, (BLK, BLK), 0)
    col = lax.broadcasted_iota(jnp.int32, (BLK, BLK), 1)
    own_ok = col <= row
    prev_ok = col >= row
    both_ok = jnp.concatenate([prev_ok, own_ok], axis=1)

    def rows(start, size, d):
        return pl.ds(start, size) if d == 1 else pl.ds(start, size, stride=d)

    def kv_rows(d, start, has_prev):
        return rows(start - d * BLK, 2 * BLK, d) if has_prev else rows(start, BLK, d)

    def scores(p, d, start, has_prev):
        q = qf[rows(start, BLK, d), :].astype(BF16)
        kk = kf[kv_rows(d, start, has_prev), :].astype(BF16)
        return jnp.where(both_ok if has_prev else own_ok, _nt(q, kk), NEG)

    def softmax(s):
        m = jnp.max(s, axis=-1, keepdims=True)
        e = jnp.exp(s - m)
        return e.astype(BF16), m, jnp.sum(e, axis=-1, keepdims=True)

    def finish(e, m, l, p, d, start, has_prev):
        vv = vf[kv_rows(d, start, has_prev), :].astype(BF16)
        dst = rows(start, BLK, d)
        m_s[p, dst, :] = jnp.broadcast_to(m, (BLK, LANE))
        l_s[p, dst, :] = jnp.broadcast_to(l, (BLK, LANE))
        acc_s[p, dst, :] = jnp.dot(e, vv, preferred_element_type=F32)

    blocks = [(p, d, r + d * BLK * n, n > 0)
              for p, (_, d) in enumerate(A_PATTERNS) for r in range(d) for n in range(SEQ // d // BLK)]
    groups = [blocks[i:i + A_GROUP] for i in range(0, len(blocks), A_GROUP)]
    s_next = [scores(*b) for b in groups[0]]
    for gi, grp in enumerate(groups):
        s_cur = s_next
        if gi + 1 < len(groups):
            s_next = [scores(*b) for b in groups[gi + 1]]
        soft = [softmax(s) for s in s_cur]
        for sf, b in zip(soft, grp):
            finish(*sf, *b)

    CH = 256

    def combine(c, carry):
        sl = pl.ds(pl.multiple_of(c * CH, CH), CH)
        ms = [m_s[p, sl, :] for p in range(3)]
        mx = jnp.maximum(jnp.maximum(ms[0], ms[1]), ms[2])
        ws = [jnp.exp(m - mx) for m in ms]
        num = ws[0] * acc_s[0, sl, :] + ws[1] * acc_s[1, sl, :] + ws[2] * acc_s[2, sl, :]
        den = ws[0] * l_s[0, sl, :] + ws[1] * l_s[1, sl, :] + ws[2] * l_s[2, sl, :]
        o_ref[sl, :] = (num / den).astype(BF16)
        return carry

    lax.fori_loop(0, SEQ // CH, combine, 0)


def _a_attn(proj):
    nh = BRANCH_W // HEAD_DIM

    def spec(col):
        return pl.BlockSpec((SEQ, HEAD_DIM), lambda b, h: (b, col // HEAD_DIM + h))

    return pl.pallas_call(
        _a_body,
        grid=(BATCH, nh),
        in_specs=[spec(COL_AQ), spec(COL_AK), spec(COL_AV)],
        out_specs=pl.BlockSpec((SEQ, HEAD_DIM), lambda b, h: (b, h)),
        out_shape=jax.ShapeDtypeStruct((ROWS, BRANCH_W), BF16),
        scratch_shapes=[pltpu.VMEM((SEQ, HEAD_DIM), F32)] * 3 + [pltpu.VMEM((3, SEQ, LANE), F32)] * 3,
        compiler_params=pltpu.CompilerParams(dimension_semantics=("arbitrary", "arbitrary"),
                                             vmem_limit_bytes=VMEM_LIMIT),
        name="mixer_a",
    )(proj, proj, proj)


QB = 256
NQB = SEQ // QB
CAUSAL_AHEAD = 2


def _rows(ref, i):
    return ref[i * QB:(i + 1) * QB, :]


def _causal_attend(q_ref, k_ref, vt_s, o_ref, bias_of):
    kk = lax.broadcasted_iota(jnp.int32, (QB, QB), 0)
    qq = lax.broadcasted_iota(jnp.int32, (QB, QB), 1)
    causal = kk <= qq
    order = list(range(NQB - 1, -1, -1))

    def scores(n):
        return _nt(k_ref[0:(n + 1) * QB, :], _rows(q_ref, n))

    pending = [scores(n) for n in order[:CAUSAL_AHEAD]]
    for i, n in enumerate(order):
        s_t = pending.pop(0)
        if i + CAUSAL_AHEAD < len(order):
            pending.append(scores(order[i + CAUSAL_AHEAD]))
        blocks = [s_t[j * QB:(j + 1) * QB] for j in range(n + 1)]
        blocks[n] = jnp.where(causal, blocks[n], NEG)
        for j in range(n):
            bias = bias_of(n, j)
            if bias is not None:
                blocks[j] = blocks[j] + bias
        m = functools.reduce(jnp.maximum, [jnp.max(b, axis=0, keepdims=True) for b in blocks])
        ps = [jnp.exp(b - m) for b in blocks]
        l = functools.reduce(jnp.add, [jnp.sum(p, axis=0, keepdims=True) for p in ps])
        p_all = jnp.concatenate([p.astype(BF16) for p in ps], axis=0)
        acc_t = jnp.dot(vt_s[:, 0:(n + 1) * QB], p_all, preferred_element_type=F32)
        o_ref[n * QB:(n + 1) * QB, :] = (acc_t * (1.0 / l)).T.astype(BF16)


TM_BP = 512


def _rms(x, g_ref):
    xf = x.astype(F32)
    return xf * lax.rsqrt(jnp.mean(xf * xf, -1, keepdims=True) + RMS_EPS) * g_ref[...]


def _bprep_body(mix_ref, mix2_ref, qn_ref, wuq_ref, kvn_ref, wukv_ref, c64, sa64, sb64, q_ref, k_ref, v_ref):
    cq = _rms(mix2_ref[:, :B_Q_RANK], qn_ref).astype(BF16)
    q = jnp.dot(cq, wuq_ref[...], preferred_element_type=F32)
    scale = (B_NOPE + B_ROPE) ** -0.5
    for h in range(B_HEADS):
        o = h * 2 * LANE
        q_ref[:, o:o + LANE] = (q[:, o:o + LANE] * scale).astype(BF16)
        q_ref[:, o + LANE:o + 2 * LANE] = (_rope64(q[:, o + LANE:o + 2 * LANE], c64[...], sa64[...], sb64[...]) * scale).astype(BF16)
    ckv = _rms(mix_ref[:, 2 * LANE:], kvn_ref).astype(BF16)
    kv = jnp.dot(ckv, wukv_ref[...], preferred_element_type=F32)
    kr = mix_ref[:, LANE:2 * LANE]
    for h in range(B_HEADS):
        o = h * 2 * LANE
        k_ref[:, o:o + LANE] = kv[:, o:o + LANE].astype(BF16)
        k_ref[:, o + LANE:o + 2 * LANE] = kr
        v_ref[:, h * LANE:(h + 1) * LANE] = kv[:, o + LANE:o + 2 * LANE].astype(BF16)


def _bprep(proj, q_norm, w_uq_p, kv_norm, w_ukv, tables):
    nt_pos = SEQ // TM_BP
    tab_spec = pl.BlockSpec((TM_BP, LANE), lambda i: (i % nt_pos, 0))
    full = lambda shape: pl.BlockSpec(shape, lambda i: (0, 0))
    return pl.pallas_call(
        _bprep_body,
        grid=(ROWS // TM_BP,),
        in_specs=[pl.BlockSpec((TM_BP, TN), lambda i: (i, COL_MIX // TN)),
                  pl.BlockSpec((TM_BP, TN), lambda i: (i, COL_MIX2 // TN)),
                  full((1, B_Q_RANK)), full((B_Q_RANK, B_HEADS * 2 * LANE)),
                  full((1, B_KV_RANK)), full((B_KV_RANK, B_HEADS * 2 * LANE))] + [tab_spec] * 3,
        out_specs=[pl.BlockSpec((TM_BP, B_HEADS * 2 * LANE), lambda i: (i, 0)),
                   pl.BlockSpec((TM_BP, B_HEADS * 2 * LANE), lambda i: (i, 0)),
                   pl.BlockSpec((TM_BP, B_HEADS * LANE), lambda i: (i, 0))],
        out_shape=[jax.ShapeDtypeStruct((ROWS, B_HEADS * 2 * LANE), BF16),
                   jax.ShapeDtypeStruct((ROWS, B_HEADS * 2 * LANE), BF16),
                   jax.ShapeDtypeStruct((ROWS, B_HEADS * LANE), BF16)],
        compiler_params=pltpu.CompilerParams(dimension_semantics=("arbitrary",),
                                             vmem_limit_bytes=VMEM_LIMIT),
        name="mla_prep",
    )(proj, proj, q_norm, w_uq_p, kv_norm, w_ukv, *tables[2:])


def _store_vt(v_ref, vt_s):
    for j in range(NQB):
        vt_s[:, j * QB:(j + 1) * QB] = _rows(v_ref, j).astype(F32).T.astype(BF16)


def _battn_body(q_ref, k_ref, v_ref, o_ref, vt_s):
    _store_vt(v_ref, vt_s)
    _causal_attend(q_ref, k_ref, vt_s, o_ref, lambda n, j: None)


def _battn(qb, kb, vb):
    return pl.pallas_call(
        _battn_body,
        grid=(BATCH, B_HEADS),
        in_specs=[pl.BlockSpec((SEQ, 2 * LANE), lambda b, h: (b, h)),
                  pl.BlockSpec((SEQ, 2 * LANE), lambda b, h: (b, h)),
                  pl.BlockSpec((SEQ, B_V), lambda b, h: (b, h))],
        out_specs=pl.BlockSpec((SEQ, B_V), lambda b, h: (b, h)),
        out_shape=jax.ShapeDtypeStruct((ROWS, BRANCH_W), BF16),
        scratch_shapes=[pltpu.VMEM((B_V, SEQ), BF16)],
        compiler_params=pltpu.CompilerParams(dimension_semantics=("arbitrary",) * 2,
                                             vmem_limit_bytes=VMEM_LIMIT),
        name="mla_attn",
    )(qb, kb, vb)


KM_ROWS = 16


def _cattn_body(q_ref, k_ref, v_ref, o_ref, vt_s, km_s, bias_s):
    _store_vt(v_ref, vt_s)
    km_s[...] = jnp.zeros_like(km_s)
    for j in range(NQB):
        km_s[j:j + 1, :] = jnp.sum(_rows(k_ref, j).astype(F32), axis=0, keepdims=True) * (1.0 / MOBA_BLOCK)
    km = km_s[...]
    km_hi = km.astype(BF16)
    km_lo = (km - km_hi.astype(F32)).astype(BF16)
    jrow = lax.broadcasted_iota(jnp.int32, (KM_ROWS, QB), 0)
    for n in range(MOBA_TOPK + 1, NQB):
        q = _rows(q_ref, n)
        g_t = _nt(km_hi, q) + _nt(km_lo, q)
        cnt = jnp.zeros((KM_ROWS, QB), F32)
        for jp in range(n):
            gj = g_t[jp:jp + 1, :]
            beats = (gj > g_t) | ((gj == g_t) & (jp < jrow))
            cnt = cnt + jnp.where(beats, 1.0, 0.0)
        bias_s[n] = jnp.where(cnt < MOBA_TOPK, 0.0, NEG)

    def bias_of(n, j):
        return bias_s[n, j:j + 1, :] if n > MOBA_TOPK else None

    _causal_attend(q_ref, k_ref, vt_s, o_ref, bias_of)


def _cattn(proj):
    def spec(col):
        return pl.BlockSpec((SEQ, HEAD_DIM), lambda b, h: (b, col // HEAD_DIM + h))

    return pl.pallas_call(
        _cattn_body,
        grid=(BATCH, BRANCH_W // HEAD_DIM),
        in_specs=[spec(COL_CQ), spec(COL_CK), spec(COL_CV)],
        out_specs=pl.BlockSpec((SEQ, HEAD_DIM), lambda b, h: (b, h)),
        out_shape=jax.ShapeDtypeStruct((ROWS, BRANCH_W), BF16),
        scratch_shapes=[pltpu.VMEM((HEAD_DIM, SEQ), BF16), pltpu.VMEM((KM_ROWS, HEAD_DIM), F32),
                        pltpu.VMEM((NQB, KM_ROWS, QB), F32)],
        compiler_params=pltpu.CompilerParams(dimension_semantics=("arbitrary",) * 2,
                                             vmem_limit_bytes=VMEM_LIMIT),
        name="moba_attn",
    )(proj, proj, proj)


D_REP = D_HEADS // D_KV_HEADS
D_GROUP = 1


def _dattn_body(sink_ref, q_ref, k_ref, v_ref, o_ref, kk_s, vv_s):
    lane_full = lax.broadcasted_iota(jnp.int32, (SEQ, LANE), 1) < D_HEAD_DIM
    kf = k_ref[...].astype(F32)
    vf = v_ref[...].astype(F32)
    kr = pltpu.roll(kf, D_HEAD_DIM, 1)
    vr = pltpu.roll(vf, D_HEAD_DIM, 1)
    kk_s[0] = jnp.where(lane_full, kf, kr).astype(BF16)
    kk_s[1] = jnp.where(lane_full, kr, kf).astype(BF16)
    vv_s[0] = jnp.where(lane_full, vf, vr).astype(BF16)
    vv_s[1] = jnp.where(lane_full, vr, vf).astype(BF16)

    lo = lax.broadcasted_iota(jnp.int32, (BLK, LANE), 1) < D_HEAD_DIM
    row = lax.broadcasted_iota(jnp.int32, (BLK, BLK), 0)
    col = lax.broadcasted_iota(jnp.int32, (BLK, BLK), 1)
    own_ok = jnp.concatenate([col <= row] * D_REP, axis=0)
    prev_ok = jnp.concatenate([col > row] * D_REP, axis=0)
    both_ok = jnp.concatenate([prev_ok, own_ok], axis=1)
    hrow = lax.broadcasted_iota(jnp.int32, (D_REP * BLK, 1), 0) // BLK

    sinks = []
    for g in range(D_KV_HEADS):
        sink = jnp.zeros((D_REP * BLK, 1), F32)
        for i in range(D_REP):
            sink = jnp.where(hrow == i, sink_ref[g * D_REP + i], sink)
        sinks.append(sink)

    def kv_rows(n):
        return pl.ds((n - 1) * BLK, 2 * BLK) if n > 0 else pl.ds(0, BLK)

    def scores(g, n):
        zero = jnp.zeros((BLK, LANE), BF16)
        parts = []
        for pg in range(2):
            c0 = (2 * g + pg) * LANE
            qp = q_ref[n * BLK:(n + 1) * BLK, c0:c0 + LANE]
            parts += [jnp.where(lo, qp, zero), jnp.where(lo, zero, qp)]
        q4 = jnp.concatenate(parts, axis=0)
        return jnp.where(both_ok if n > 0 else own_ok, _nt(q4, kk_s[g, kv_rows(n), :]), NEG)

    def softmax(s, g, n):
        m2 = jnp.maximum(jnp.max(s, axis=-1, keepdims=True), sinks[g])
        e = jnp.exp(s - m2)
        den = jnp.sum(e, axis=-1, keepdims=True) + jnp.exp(sinks[g] - m2)
        return e.astype(BF16), 1.0 / den

    def finish(e, inv, g, n):
        o4 = jnp.dot(e, vv_s[g, kv_rows(n), :], preferred_element_type=F32) * inv
        for pg in range(2):
            c0 = (2 * g + pg) * LANE
            o_ref[n * BLK:(n + 1) * BLK, c0:c0 + LANE] = jnp.where(
                lo, o4[(2 * pg) * BLK:(2 * pg + 1) * BLK], o4[(2 * pg + 1) * BLK:(2 * pg + 2) * BLK]).astype(BF16)

    blocks = [(g, n) for g in range(D_KV_HEADS) for n in range(SEQ // BLK)]
    groups = [blocks[i:i + D_GROUP] for i in range(0, len(blocks), D_GROUP)]
    s_next = [scores(*b) for b in groups[0]]
    for gi, grp in enumerate(groups):
        s_cur = s_next
        if gi + 1 < len(groups):
            s_next = [scores(*b) for b in groups[gi + 1]]
        soft = [softmax(s, *b) for s, b in zip(s_cur, grp)]
        for sf, b in zip(soft, grp):
            finish(*sf, *b)


def _dattn(proj, sinks):
    return pl.pallas_call(
        _dattn_body,
        grid=(BATCH,),
        in_specs=[pl.BlockSpec(memory_space=pltpu.SMEM),
                  pl.BlockSpec((SEQ, BRANCH_W), lambda b: (b, COL_DQ // BRANCH_W)),
                  pl.BlockSpec((SEQ, LANE), lambda b: (b, COL_DK // LANE)),
                  pl.BlockSpec((SEQ, LANE), lambda b: (b, COL_DV // LANE))],
        out_specs=pl.BlockSpec((SEQ, BRANCH_W), lambda b: (b, 0)),
        out_shape=jax.ShapeDtypeStruct((ROWS, BRANCH_W), BF16),
        scratch_shapes=[pltpu.VMEM((D_KV_HEADS, SEQ, LANE), BF16)] * 2,
        compiler_params=pltpu.CompilerParams(dimension_semantics=("arbitrary",),
                                             vmem_limit_bytes=VMEM_LIMIT),
        name="sink_swa",
    )(sinks, proj, proj, proj)


TM_OUT = 512
OUT_CHUNK = 256


def _outproj_body(ya, yb, yc, yd, ga, gb, gc, gd, x_ref, w_ref, bn_ref, g_ref, b_ref, of_ref, ob_ref):
    for c in range(TM_OUT // OUT_CHUNK):
        rows = slice(c * OUT_CHUNK, (c + 1) * OUT_CHUNK)
        acc = DN_ALPHA * x_ref[rows, :]
        for i, (y_ref, gate_ref) in enumerate(((ya, ga), (yb, gb), (yc, gc), (yd, gd))):
            sl = slice(i * BRANCH_W, (i + 1) * BRANCH_W)
            y = y_ref[rows, :].astype(F32)
            y = y * lax.rsqrt(jnp.mean(y * y, -1, keepdims=True) + RMS_EPS) * bn_ref[:, sl]
            gt = gate_ref[rows, :].astype(F32)
            y = (y * (gt * jax.nn.sigmoid(gt))).astype(BF16)
            acc = acc + jnp.dot(y, w_ref[sl, :], preferred_element_type=F32)
        mu = jnp.mean(acc, -1, keepdims=True)
        xc = acc - mu
        var = jnp.mean(xc * xc, -1, keepdims=True)
        out = xc * lax.rsqrt(var + LN_EPS) * g_ref[...] + b_ref[...]
        of_ref[rows, :] = out
        ob_ref[rows, :] = out.astype(BF16)


def _outproj(ya, yb, yc, yd, proj, xf, w_out_all, layer, bn, ln_g, ln_b):
    y_spec = pl.BlockSpec((TM_OUT, BRANCH_W), lambda i: (i, 0))
    gate_specs = [pl.BlockSpec((TM_OUT, BRANCH_W), functools.partial(lambda i, c: (i, c), c=COL_GATE // BRANCH_W + g))
                  for g in range(4)]
    row_spec = pl.BlockSpec((1, D_MODEL), lambda i: (0, 0))
    x_spec = pl.BlockSpec((TM_OUT, D_MODEL), lambda i: (i, 0))
    return pl.pallas_call(
        _outproj_body,
        grid=(ROWS // TM_OUT,),
        in_specs=[y_spec] * 4 + gate_specs + [x_spec, pl.BlockSpec((None, D_MODEL, D_MODEL), lambda i: (layer, 0, 0)),
                                              row_spec, row_spec, row_spec],
        out_specs=[x_spec, x_spec],
        out_shape=[jax.ShapeDtypeStruct((ROWS, D_MODEL), F32), jax.ShapeDtypeStruct((ROWS, D_MODEL), BF16)],
        compiler_params=pltpu.CompilerParams(dimension_semantics=("arbitrary",),
                                             vmem_limit_bytes=VMEM_LIMIT),
        name="outproj",
    )(ya, yb, yc, yd, proj, proj, proj, proj, xf, w_out_all, bn, ln_g, ln_b)


def kernel(x, w_in, q_norm, w_uq, kv_norm, w_ukv, sinks, branch_norm, w_out, ln_gamma, ln_beta):
    assert x.shape == (BATCH, SEQ, D_MODEL) and w_in.shape == (DEPTH, D_MODEL, sum(SPLIT_SIZES))
    tables = _rope_tables()
    w_t = jnp.swapaxes(w_in, 1, 2)
    w_uq_p = _permute_w_uq(w_uq)
    w_ukv_b = w_ukv.astype(BF16)
    w_out_b = w_out.astype(BF16)
    xf = x.reshape(ROWS, D_MODEL)
    xb = xf.astype(BF16)
    for l in range(DEPTH):
        proj = _inproj(xb, w_t, l, tables)
        ya = _a_attn(proj)
        qb, kb, vb = _bprep(proj, q_norm[l][None], w_uq_p[l], kv_norm[l][None], w_ukv_b[l], tables)
        yb = _battn(qb, kb, vb)
        yc = _cattn(proj)
        yd = _dattn(proj, sinks[l])
        xf, xb = _outproj(ya, yb, yc, yd, proj, xf, w_out_b, l, branch_norm[l][None], ln_gamma[l][None], ln_beta[l][None])
    return xf.reshape(BATCH, SEQ, D_MODEL)
```

```python
import functools
import math

import numpy as np
import jax
import jax.numpy as jnp
from jax import lax
from jax.experimental import pallas as pl
from jax.experimental.pallas import tpu as pltpu

F32 = jnp.float32
BF16 = jnp.bfloat16

D_MODEL = 2048
BATCH = 4
SEQ = 2048
DEPTH = 4
BRANCH_W = 512
HEAD_DIM = 128
A_PATTERNS = ((128, 1), (512, 4), (2048, 16))
B_HEADS = 4
B_Q_RANK = 384
B_KV_RANK = 256
B_NOPE = 128
B_ROPE = 64
B_V = 128
MOBA_BLOCK = 256
MOBA_TOPK = 3
D_HEAD_DIM = 64
D_HEADS = 8
D_KV_HEADS = 2
ROPE_THETA = 10000.0
RMS_EPS = 1e-6
LN_EPS = 1e-5
NEG = -1e30
DN_ALPHA = (2 * DEPTH) ** 0.25
SPLIT_SIZES = (512, 512, 512, 384, 256, 64, 512, 512, 512, 512, 128, 128, 2048)

LANE = 128
BLK = 128
ROWS = BATCH * SEQ

TN = 512
COL_AQ, COL_AK, COL_CQ, COL_CK, COL_DQ = 0, 512, 1024, 1536, 2048
COL_MIX = 2560
COL_DK, COL_KR, COL_CKV = 2560, 2688, 2816
COL_AV, COL_CV = 3072, 3584
COL_MIX2 = 4096
COL_BCQ, COL_DV = 4096, 4480
COL_GATE = 4608
N_PROJ = 6656

VMEM_LIMIT = 48 * 1024 * 1024


def _nt(a, b):
    return lax.dot_general(a, b, (((1,), (1,)), ((), ())), preferred_element_type=F32)


_OFFS = np.cumsum((0,) + SPLIT_SIZES)
(_O_AQ, _O_AK, _O_AV, _O_BCQ, _O_BCKV, _O_BKR, _O_CQ, _O_CK, _O_CV, _O_DQ, _O_DK, _O_DV, _O_GATE) = (int(o) for o in _OFFS[:-1])


def _src_rows():
    def run(start):
        return [start + LANE * g for g in range(TN // LANE)]

    tiles = [run(_O_AQ), run(_O_AK), run(_O_CQ), run(_O_CK), run(_O_DQ),
             [_O_DK, _O_BKR, _O_BCKV, _O_BCKV + LANE],
             run(_O_AV), run(_O_CV),
             [_O_BCQ, _O_BCQ + LANE, _O_BCQ + 2 * LANE, _O_DV]]
    tiles += [run(_O_GATE + TN * t) for t in range(4)]
    return np.asarray(tiles, np.int32).reshape(-1)


def _permute_w_uq(w_uq):
    L = w_uq.shape[0]
    w = w_uq.reshape(L, B_Q_RANK, B_HEADS, B_NOPE + B_ROPE)
    w = jnp.pad(w, ((0, 0), (0, 0), (0, 0), (0, 2 * LANE - B_NOPE - B_ROPE)))
    return w.reshape(L, B_Q_RANK, B_HEADS * 2 * LANE).astype(BF16)


def _rope_angles(dim):
    pos = jnp.arange(SEQ, dtype=jnp.int32)
    inv = ROPE_THETA ** (-jnp.arange(0, dim, 2, dtype=jnp.float32) / dim)
    ang = pos.astype(jnp.float32)[:, None] * inv[None, :]
    return jnp.cos(ang), jnp.sin(ang)


def _rope_tables():
    c, s = _rope_angles(HEAD_DIM)
    cos128 = jnp.concatenate([c, c], -1)
    sin128 = jnp.concatenate([-s, s], -1)
    c, s = _rope_angles(D_HEAD_DIM)
    z = jnp.zeros_like(s)
    cos64 = jnp.concatenate([c, c, c, c], -1)
    sin64a = jnp.concatenate([z, s, z, s], -1)
    sin64b = jnp.concatenate([-s, z, -s, z], -1)
    return cos128, sin128, cos64, sin64a, sin64b


TM_IN = 1024
IN_CHUNK = 256


def _rope128(x, c, s):
    return x * c + pltpu.roll(x, 64, 1) * s


def _rope64(x, c, sa, sb):
    return x * c + pltpu.roll(x, 32, 1) * sa + pltpu.roll(x, 96, 1) * sb


N_GRP = TN // LANE


def _inproj_body(tbl_ref, x_ref, w0, w1, w2, w3, c128, s128, c64, sa64, sb64, o_ref, wt_s):
    del tbl_ref
    j = pl.program_id(0)
    i = pl.program_id(1)

    @pl.when(i == 0)
    def _():
        for g, w in enumerate((w0, w1, w2, w3)):
            wt_s[g * LANE:(g + 1) * LANE, :] = w[0].astype(BF16)

        @pl.when(j == COL_MIX // TN)
        def _():
            wt_s[LANE + B_ROPE:2 * LANE, :] = jnp.zeros((LANE - B_ROPE, D_MODEL), BF16)

    pos0 = (i % (SEQ // TM_IN)) * TM_IN
    groups = [slice(g * LANE, (g + 1) * LANE) for g in range(N_GRP)]

    def row_chunks(epilogue):
        def piece(c):
            return _nt(x_ref[c * IN_CHUNK:(c + 1) * IN_CHUNK, :], wt_s[...])

        acc_next = piece(0)
        for c in range(TM_IN // IN_CHUNK):
            acc = acc_next
            if c + 1 < TM_IN // IN_CHUNK:
                acc_next = piece(c + 1)
            epilogue(acc, slice(c * IN_CHUNK, (c + 1) * IN_CHUNK),
                     pl.ds(pl.multiple_of(pos0 + c * IN_CHUNK, IN_CHUNK), IN_CHUNK))

    @pl.when(j < 4)
    def _():
        scale = jnp.where((j == 0) | (j == 2), HEAD_DIM ** -0.5, 1.0).astype(F32)

        def epilogue(acc, rows, pos):
            c, s = c128[pos, :], s128[pos, :]
            for sl in groups:
                o_ref[rows, sl] = (_rope128(acc[:, sl], c, s) * scale).astype(BF16)

        row_chunks(epilogue)

    @pl.when(j == 4)
    def _():
        def epilogue(acc, rows, pos):
            c, sa, sb = c64[pos, :], sa64[pos, :], sb64[pos, :]
            for sl in groups:
                o_ref[rows, sl] = (_rope64(acc[:, sl], c, sa, sb) * (D_HEAD_DIM ** -0.5)).astype(BF16)

        row_chunks(epilogue)

    @pl.when(j == 5)
    def _():
        def epilogue(acc, rows, pos):
            c, sa, sb = c64[pos, :], sa64[pos, :], sb64[pos, :]
            for sl in groups[:2]:
                o_ref[rows, sl] = _rope64(acc[:, sl], c, sa, sb).astype(BF16)
            o_ref[rows, 2 * LANE:] = acc[:, 2 * LANE:].astype(BF16)

        row_chunks(epilogue)

    @pl.when(j > 5)
    def _():
        def epilogue(acc, rows, pos):
            o_ref[rows, :] = acc.astype(BF16)

        row_chunks(epilogue)


def _inproj(xb, w_t, layer, tables):
    def w_spec(g):
        return pl.BlockSpec((pl.Element(1), pl.Element(LANE), pl.Element(D_MODEL)),
                            lambda j, i, tbl: (layer, pl.multiple_of(tbl[j * N_GRP + g], B_ROPE), 0))

    tab_spec = pl.BlockSpec((SEQ, LANE), lambda j, i, tbl: (0, 0))
    return pl.pallas_call(
        _inproj_body,
        grid_spec=pltpu.PrefetchScalarGridSpec(
            num_scalar_prefetch=1,
            grid=(N_PROJ // TN, ROWS // TM_IN),
            in_specs=[pl.BlockSpec((TM_IN, D_MODEL), lambda j, i, tbl: (i, 0))]
            + [w_spec(g) for g in range(N_GRP)] + [tab_spec] * 5,
            out_specs=pl.BlockSpec((TM_IN, TN), lambda j, i, tbl: (i, j)),
            scratch_shapes=[pltpu.VMEM((TN, D_MODEL), BF16)]),
        out_shape=jax.ShapeDtypeStruct((ROWS, N_PROJ), BF16),
        compiler_params=pltpu.CompilerParams(dimension_semantics=("arbitrary", "arbitrary"),
                                             vmem_limit_bytes=VMEM_LIMIT),
        name="inproj",
    )(jnp.asarray(_src_rows()), xb, w_t, w_t, w_t, w_t, *tables)


A_GROUP = 4
def _a_body(q_ref, k_ref, v_ref, o_ref, qf, kf, vf, m_s, l_s, acc_s):
    qf[...] = q_ref[...].astype(F32)
    kf[...] = k_ref[...].astype(F32)
    vf[...] = v_ref[...].astype(F32)
    row = lax.broadcasted_iota(jnp.int32, (BLK, BLK), 0)
    col = lax.broadcasted_iota(jnp.int32, (BLK, BLK), 1)
    own_ok = col <= row
    prev_ok = col >= row
    both_ok = jnp.concatenate([prev_ok, own_ok], axis=1)

    def rows(start, size, d):
        return pl.ds(start, size) if d == 1 else pl.ds(start, size, stride=d)

    def kv_rows(d, start, has_prev):
        return rows(start - d * BLK, 2 * BLK, d) if has_prev else rows(start, BLK, d)

    def scores(p, d, start, has_prev):
        q = qf[rows(start, BLK, d), :].astype(BF16)
        kk = kf[kv_rows(d, start, has_prev), :].astype(BF16)
        return jnp.where(both_ok if has_prev else own_ok, _nt(q, kk), NEG)

    def softmax(s):
        m = jnp.max(s, axis=-1, keepdims=True)
        e = jnp.exp(s - m)
        return e.astype(BF16), m, jnp.sum(e, axis=-1, keepdims=True)

    def finish(e, m, l, p, d, start, has_prev):
        vv = vf[kv_rows(d, start, has_prev), :].astype(BF16)
        dst = rows(start, BLK, d)
        m_s[p, dst, :] = jnp.broadcast_to(m, (BLK, LANE))
        l_s[p, dst, :] = jnp.broadcast_to(l, (BLK, LANE))
        acc_s[p, dst, :] = jnp.dot(e, vv, preferred_element_type=F32)

    blocks = [(p, d, r + d * BLK * n, n > 0)
              for p, (_, d) in enumerate(A_PATTERNS) for r in range(d) for n in range(SEQ // d // BLK)]
    groups = [blocks[i:i + A_GROUP] for i in range(0, len(blocks), A_GROUP)]
    s_next = [scores(*b) for b in groups[0]]
    for gi, grp in enumerate(groups):
        s_cur = s_next
        if gi + 1 < len(groups):
            s_next = [scores(*b) for b in groups[gi + 1]]
        soft = [softmax(s) for s in s_cur]
        for sf, b in zip(soft, grp):
            finish(*sf, *b)

    CH = 256

    def combine(c, carry):
        sl = pl.ds(pl.multiple_of(c * CH, CH), CH)
        ms = [m_s[p, sl, :] for p in range(3)]
        mx = jnp.maximum(jnp.maximum(ms[0], ms[1]), ms[2])
        ws = [jnp.exp(m - mx) for m in ms]
        num = ws[0] * acc_s[0, sl, :] + ws[1] * acc_s[1, sl, :] + ws[2] * acc_s[2, sl, :]
        den = ws[0] * l_s[0, sl, :] + ws[1] * l_s[1, sl, :] + ws[2] * l_s[2, sl, :]
        o_ref[sl, :] = (num / den).astype(BF16)
        return carry

    lax.fori_loop(0, SEQ // CH, combine, 0)


def _a_attn(proj):
    nh = BRANCH_W // HEAD_DIM

    def spec(col):
        return pl.BlockSpec((SEQ, HEAD_DIM), lambda b, h: (b, col // HEAD_DIM + h))

    return pl.pallas_call(
        _a_body,
        grid=(BATCH, nh),
        in_specs=[spec(COL_AQ), spec(COL_AK), spec(COL_AV)],
        out_specs=pl.BlockSpec((SEQ, HEAD_DIM), lambda b, h: (b, h)),
        out_shape=jax.ShapeDtypeStruct((ROWS, BRANCH_W), BF16),
        scratch_shapes=[pltpu.VMEM((SEQ, HEAD_DIM), F32)] * 3 + [pltpu.VMEM((3, SEQ, LANE), F32)] * 3,
        compiler_params=pltpu.CompilerParams(dimension_semantics=("arbitrary", "arbitrary"),
                                             vmem_limit_bytes=VMEM_LIMIT),
        name="mixer_a",
    )(proj, proj, proj)


QB = 256
NQB = SEQ // QB
CAUSAL_AHEAD = 2


def _rows(ref, i):
    return ref[i * QB:(i + 1) * QB, :]


def _causal_attend(q_ref, k_ref, vt_s, o_ref, bias_of):
    kk = lax.broadcasted_iota(jnp.int32, (QB, QB), 0)
    qq = lax.broadcasted_iota(jnp.int32, (QB, QB), 1)
    causal = kk <= qq
    order = list(range(NQB - 1, -1, -1))

    def scores(n):
        return _nt(k_ref[0:(n + 1) * QB, :], _rows(q_ref, n))

    pending = [scores(n) for n in order[:CAUSAL_AHEAD]]
    for i, n in enumerate(order):
        s_t = pending.pop(0)
        if i + CAUSAL_AHEAD < len(order):
            pending.append(scores(order[i + CAUSAL_AHEAD]))
        blocks = [s_t[j * QB:(j + 1) * QB] for j in range(n + 1)]
        blocks[n] = jnp.where(causal, blocks[n], NEG)
        for j in range(n):
            bias = bias_of(n, j)
            if bias is not None:
                blocks[j] = blocks[j] + bias
        m = functools.reduce(jnp.maximum, [jnp.max(b, axis=0, keepdims=True) for b in blocks])
        ps = [jnp.exp(b - m) for b in blocks]
        l = functools.reduce(jnp.add, [jnp.sum(p, axis=0, keepdims=True) for p in ps])
        p_all = jnp.concatenate([p.astype(BF16) for p in ps], axis=0)
        acc_t = jnp.dot(vt_s[:, 0:(n + 1) * QB], p_all, preferred_element_type=F32)
        o_ref[n * QB:(n + 1) * QB, :] = (acc_t * (1.0 / l)).T.astype(BF16)


TM_BP = 512


def _rms(x, g_ref):
    xf = x.astype(F32)
    return xf * lax.rsqrt(jnp.mean(xf * xf, -1, keepdims=True) + RMS_EPS) * g_ref[...]


def _bprep_body(mix_ref, mix2_ref, qn_ref, wuq_ref, kvn_ref, wukv_ref, c64, sa64, sb64, q_ref, k_ref, v_ref):
    cq = _rms(mix2_ref[:, :B_Q_RANK], qn_ref).astype(BF16)
    q = jnp.dot(cq, wuq_ref[...], preferred_element_type=F32)
    scale = (B_NOPE + B_ROPE) ** -0.5
    pos = pl.ds(pl.multiple_of((pl.program_id(0) % (SEQ // TM_BP)) * TM_BP, TM_BP), TM_BP)
    c, sa, sb = c64[pos, :], sa64[pos, :], sb64[pos, :]
    for h in range(B_HEADS):
        o = h * 2 * LANE
        q_ref[:, o:o + LANE] = (q[:, o:o + LANE] * scale).astype(BF16)
        q_ref[:, o + LANE:o + 2 * LANE] = (_rope64(q[:, o + LANE:o + 2 * LANE], c, sa, sb) * scale).astype(BF16)
    ckv = _rms(mix_ref[:, 2 * LANE:], kvn_ref).astype(BF16)
    kv = jnp.dot(ckv, wukv_ref[...], preferred_element_type=F32)
    kr = mix_ref[:, LANE:2 * LANE]
    for h in range(B_HEADS):
        o = h * 2 * LANE
        k_ref[:, o:o + LANE] = kv[:, o:o + LANE].astype(BF16)
        k_ref[:, o + LANE:o + 2 * LANE] = kr
        v_ref[:, h * LANE:(h + 1) * LANE] = kv[:, o + LANE:o + 2 * LANE].astype(BF16)


def _bprep(proj, q_norm, w_uq_p, kv_norm, w_ukv, tables):
    tab_spec = pl.BlockSpec((SEQ, LANE), lambda i: (0, 0))
    full = lambda shape: pl.BlockSpec(shape, lambda i: (0, 0))
    return pl.pallas_call(
        _bprep_body,
        grid=(ROWS // TM_BP,),
        in_specs=[pl.BlockSpec((TM_BP, TN), lambda i: (i, COL_MIX // TN)),
                  pl.BlockSpec((TM_BP, TN), lambda i: (i, COL_MIX2 // TN)),
                  full((1, B_Q_RANK)), full((B_Q_RANK, B_HEADS * 2 * LANE)),
                  full((1, B_KV_RANK)), full((B_KV_RANK, B_HEADS * 2 * LANE))] + [tab_spec] * 3,
        out_specs=[pl.BlockSpec((TM_BP, B_HEADS * 2 * LANE), lambda i: (i, 0)),
                   pl.BlockSpec((TM_BP, B_HEADS * 2 * LANE), lambda i: (i, 0)),
                   pl.BlockSpec((TM_BP, B_HEADS * LANE), lambda i: (i, 0))],
        out_shape=[jax.ShapeDtypeStruct((ROWS, B_HEADS * 2 * LANE), BF16),
                   jax.ShapeDtypeStruct((ROWS, B_HEADS * 2 * LANE), BF16),
                   jax.ShapeDtypeStruct((ROWS, B_HEADS * LANE), BF16)],
        compiler_params=pltpu.CompilerParams(dimension_semantics=("arbitrary",),
                                             vmem_limit_bytes=VMEM_LIMIT),
        name="mla_prep",
    )(proj, proj, q_norm, w_uq_p, kv_norm, w_ukv, *tables[2:])


def _store_vt(v_ref, vt_s):
    for j in range(NQB):
        vt_s[:, j * QB:(j + 1) * QB] = _rows(v_ref, j).astype(F32).T.astype(BF16)


def _battn_body(q_ref, k_ref, v_ref, o_ref, vt_s):
    _store_vt(v_ref, vt_s)
    _causal_attend(q_ref, k_ref, vt_s, o_ref, lambda n, j: None)


def _battn(qb, kb, vb):
    return pl.pallas_call(
        _battn_body,
        grid=(BATCH, B_HEADS),
        in_specs=[pl.BlockSpec((SEQ, 2 * LANE), lambda b, h: (b, h)),
                  pl.BlockSpec((SEQ, 2 * LANE), lambda b, h: (b, h)),
                  pl.BlockSpec((SEQ, B_V), lambda b, h: (b, h))],
        out_specs=pl.BlockSpec((SEQ, B_V), lambda b, h: (b, h)),
        out_shape=jax.ShapeDtypeStruct((ROWS, BRANCH_W), BF16),
        scratch_shapes=[pltpu.VMEM((B_V, SEQ), BF16)],
        compiler_params=pltpu.CompilerParams(dimension_semantics=("arbitrary",) * 2,
                                             vmem_limit_bytes=VMEM_LIMIT),
        name="mla_attn",
    )(qb, kb, vb)


KM_ROWS = 16


def _cattn_body(q_ref, k_ref, v_ref, o_ref, vt_s, km_s, bias_s):
    _store_vt(v_ref, vt_s)
    km_s[...] = jnp.zeros_like(km_s)
    for j in range(NQB):
        km_s[j:j + 1, :] = jnp.sum(_rows(k_ref, j).astype(F32), axis=0, keepdims=True) * (1.0 / MOBA_BLOCK)
    km = km_s[...]
    km_hi = km.astype(BF16)
    km_lo = (km - km_hi.astype(F32)).astype(BF16)
    jrow = lax.broadcasted_iota(jnp.int32, (KM_ROWS, QB), 0)
    for n in range(MOBA_TOPK + 1, NQB):
        q = _rows(q_ref, n)
        g_t = _nt(km_hi, q) + _nt(km_lo, q)
        cnt = jnp.zeros((KM_ROWS, QB), F32)
        for jp in range(n):
            gj = g_t[jp:jp + 1, :]
            beats = (gj > g_t) | ((gj == g_t) & (jp < jrow))
            cnt = cnt + jnp.where(beats, 1.0, 0.0)
        bias_s[n] = jnp.where(cnt < MOBA_TOPK, 0.0, NEG)

    def bias_of(n, j):
        return bias_s[n, j:j + 1, :] if n > MOBA_TOPK else None

    _causal_attend(q_ref, k_ref, vt_s, o_ref, bias_of)


def _cattn(proj):
    def spec(col):
        return pl.BlockSpec((SEQ, HEAD_DIM), lambda b, h: (b, col // HEAD_DIM + h))

    return pl.pallas_call(
        _cattn_body,
        grid=(BATCH, BRANCH_W // HEAD_DIM),
        in_specs=[spec(COL_CQ), spec(COL_CK), spec(COL_CV)],
        out_specs=pl.BlockSpec((SEQ, HEAD_DIM), lambda b, h: (b, h)),
        out_shape=jax.ShapeDtypeStruct((ROWS, BRANCH_W), BF16),
        scratch_shapes=[pltpu.VMEM((HEAD_DIM, SEQ), BF16), pltpu.VMEM((KM_ROWS, HEAD_DIM), F32),
                        pltpu.VMEM((NQB, KM_ROWS, QB), F32)],
        compiler_params=pltpu.CompilerParams(dimension_semantics=("arbitrary",) * 2,
                                             vmem_limit_bytes=VMEM_LIMIT),
        name="moba_attn",
    )(proj, proj, proj)


D_REP = D_HEADS // D_KV_HEADS
D_GROUP = 1


def _dattn_body(sink_ref, q_ref, k_ref, v_ref, o_ref, kk_s, vv_s):
    lane_full = lax.broadcasted_iota(jnp.int32, (SEQ, LANE), 1) < D_HEAD_DIM
    kf = k_ref[...].astype(F32)
    vf = v_ref[...].astype(F32)
    kr = pltpu.roll(kf, D_HEAD_DIM, 1)
    vr = pltpu.roll(vf, D_HEAD_DIM, 1)
    kk_s[0] = jnp.where(lane_full, kf, kr).astype(BF16)
    kk_s[1] = jnp.where(lane_full, kr, kf).astype(BF16)
    vv_s[0] = jnp.where(lane_full, vf, vr).astype(BF16)
    vv_s[1] = jnp.where(lane_full, vr, vf).astype(BF16)

    lo = lax.broadcasted_iota(jnp.int32, (BLK, LANE), 1) < D_HEAD_DIM
    row = lax.broadcasted_iota(jnp.int32, (BLK, BLK), 0)
    col = lax.broadcasted_iota(jnp.int32, (BLK, BLK), 1)
    own_ok = jnp.concatenate([col <= row] * D_REP, axis=0)
    prev_ok = jnp.concatenate([col > row] * D_REP, axis=0)
    both_ok = jnp.concatenate([prev_ok, own_ok], axis=1)
    hrow = lax.broadcasted_iota(jnp.int32, (D_REP * BLK, 1), 0) // BLK

    sinks = []
    for g in range(D_KV_HEADS):
        sink = jnp.zeros((D_REP * BLK, 1), F32)
        for i in range(D_REP):
            sink = jnp.where(hrow == i, sink_ref[g * D_REP + i], sink)
        sinks.append(sink)

    def kv_rows(n):
        return pl.ds((n - 1) * BLK, 2 * BLK) if n > 0 else pl.ds(0, BLK)

    def scores(g, n):
        zero = jnp.zeros((BLK, LANE), BF16)
        parts = []
        for pg in range(2):
            c0 = (2 * g + pg) * LANE
            qp = q_ref[n * BLK:(n + 1) * BLK, c0:c0 + LANE]
            parts += [jnp.where(lo, qp, zero), jnp.where(lo, zero, qp)]
        q4 = jnp.concatenate(parts, axis=0)
        return jnp.where(both_ok if n > 0 else own_ok, _nt(q4, kk_s[g, kv_rows(n), :]), NEG)

    def softmax(s, g, n):
        m2 = jnp.maximum(jnp.max(s, axis=-1, keepdims=True), sinks[g])
        e = jnp.exp(s - m2)
        den = jnp.sum(e, axis=-1, keepdims=True) + jnp.exp(sinks[g] - m2)
        return e.astype(BF16), 1.0 / den

    def finish(e, inv, g, n):
        o4 = jnp.dot(e, vv_s[g, kv_rows(n), :], preferred_element_type=F32) * inv
        for pg in range(2):
            c0 = (2 * g + pg) * LANE
            o_ref[n * BLK:(n + 1) * BLK, c0:c0 + LANE] = jnp.where(
                lo, o4[(2 * pg) * BLK:(2 * pg + 1) * BLK], o4[(2 * pg + 1) * BLK:(2 * pg + 2) * BLK]).astype(BF16)

    blocks = [(g, n) for g in range(D_KV_HEADS) for n in range(SEQ // BLK)]
    groups = [blocks[i:i + D_GROUP] for i in range(0, len(blocks), D_GROUP)]
    s_next = [scores(*b) for b in groups[0]]
    for gi, grp in enumerate(groups):
        s_cur = s_next
        if gi + 1 < len(groups):
            s_next = [scores(*b) for b in groups[gi + 1]]
        soft = [softmax(s, *b) for s, b in zip(s_cur, grp)]
        for sf, b in zip(soft, grp):
            finish(*sf, *b)


def _dattn(proj, sinks):
    return pl.pallas_call(
        _dattn_body,
        grid=(BATCH,),
        in_specs=[pl.BlockSpec(memory_space=pltpu.SMEM),
                  pl.BlockSpec((SEQ, BRANCH_W), lambda b: (b, COL_DQ // BRANCH_W)),
                  pl.BlockSpec((SEQ, LANE), lambda b: (b, COL_DK // LANE)),
                  pl.BlockSpec((SEQ, LANE), lambda b: (b, COL_DV // LANE))],
        out_specs=pl.BlockSpec((SEQ, BRANCH_W), lambda b: (b, 0)),
        out_shape=jax.ShapeDtypeStruct((ROWS, BRANCH_W), BF16),
        scratch_shapes=[pltpu.VMEM((D_KV_HEADS, SEQ, LANE), BF16)] * 2,
        compiler_params=pltpu.CompilerParams(dimension_semantics=("arbitrary",),
                                             vmem_limit_bytes=VMEM_LIMIT),
        name="sink_swa",
    )(sinks, proj, proj, proj)


TM_OUT = 512
OUT_CHUNK = 256


def _outproj_body(ya, yb, yc, yd, ga, gb, gc, gd, x_ref, w_ref, bn_ref, g_ref, b_ref, of_ref, ob_ref):
    for c in range(TM_OUT // OUT_CHUNK):
        rows = slice(c * OUT_CHUNK, (c + 1) * OUT_CHUNK)
        acc = DN_ALPHA * x_ref[rows, :]
        for i, (y_ref, gate_ref) in enumerate(((ya, ga), (yb, gb), (yc, gc), (yd, gd))):
            sl = slice(i * BRANCH_W, (i + 1) * BRANCH_W)
            y = y_ref[rows, :].astype(F32)
            y = y * lax.rsqrt(jnp.mean(y * y, -1, keepdims=True) + RMS_EPS) * bn_ref[:, sl]
            gt = gate_ref[rows, :].astype(F32)
            y = (y * (gt * jax.nn.sigmoid(gt))).astype(BF16)
            acc = acc + jnp.dot(y, w_ref[sl, :], preferred_element_type=F32)
        mu = jnp.mean(acc, -1, keepdims=True)
        xc = acc - mu
        var = jnp.mean(xc * xc, -1, keepdims=True)
        out = xc * lax.rsqrt(var + LN_EPS) * g_ref[...] + b_ref[...]
        of_ref[rows, :] = out
        ob_ref[rows, :] = out.astype(BF16)


def _outproj(ya, yb, yc, yd, proj, xf, w_out_all, layer, bn, ln_g, ln_b):
    y_spec = pl.BlockSpec((TM_OUT, BRANCH_W), lambda i: (i, 0))
    gate_specs = [pl.BlockSpec((TM_OUT, BRANCH_W), functools.partial(lambda i, c: (i, c), c=COL_GATE // BRANCH_W + g))
                  for g in range(4)]
    row_spec = pl.BlockSpec((1, D_MODEL), lambda i: (0, 0))
    x_spec = pl.BlockSpec((TM_OUT, D_MODEL), lambda i: (i, 0))
    return pl.pallas_call(
        _outproj_body,
        grid=(ROWS // TM_OUT,),
        in_specs=[y_spec] * 4 + gate_specs + [x_spec, pl.BlockSpec((None, D_MODEL, D_MODEL), lambda i: (layer, 0, 0)),
                                              row_spec, row_spec, row_spec],
        out_specs=[x_spec, x_spec],
        out_shape=[jax.ShapeDtypeStruct((ROWS, D_MODEL), F32), jax.ShapeDtypeStruct((ROWS, D_MODEL), BF16)],
        compiler_params=pltpu.CompilerParams(dimension_semantics=("arbitrary",),
                                             vmem_limit_bytes=VMEM_LIMIT),
        name="outproj",
    )(ya, yb, yc, yd, proj, proj, proj, proj, xf, w_out_all, bn, ln_g, ln_b)


def kernel(x, w_in, q_norm, w_uq, kv_norm, w_ukv, sinks, branch_norm, w_out, ln_gamma, ln_beta):
    assert x.shape == (BATCH, SEQ, D_MODEL) and w_in.shape == (DEPTH, D_MODEL, sum(SPLIT_SIZES))
    tables = _rope_tables()
    w_t = jnp.swapaxes(w_in, 1, 2)
    w_uq_p = _permute_w_uq(w_uq)
    w_ukv_b = w_ukv.astype(BF16)
    w_out_b = w_out.astype(BF16)
    xf = x.reshape(ROWS, D_MODEL)
    xb = xf.astype(BF16)
    for l in range(DEPTH):
        proj = _inproj(xb, w_t, l, tables)
        ya = _a_attn(proj)
        qb, kb, vb = _bprep(proj, q_norm[l][None], w_uq_p[l], kv_norm[l][None], w_ukv_b[l], tables)
        yb = _battn(qb, kb, vb)
        yc = _cattn(proj)
        yd = _dattn(proj, sinks[l])
        xf, xb = _outproj(ya, yb, yc, yd, proj, xf, w_out_b, l, branch_norm[l][None], ln_gamma[l][None], ln_beta[l][None])
    return xf.reshape(BATCH, SEQ, D_MODEL)
```

```python
import functools
import math

import numpy as np
import jax
import jax.numpy as jnp
from jax import lax
from jax.experimental import pallas as pl
from jax.experimental.pallas import tpu as pltpu

F32 = jnp.float32
BF16 = jnp.bfloat16

D_MODEL = 2048
BATCH = 4
SEQ = 2048
DEPTH = 4
BRANCH_W = 512
HEAD_DIM = 128
A_PATTERNS = ((128, 1), (512, 4), (2048, 16))
B_HEADS = 4
B_Q_RANK = 384
B_KV_RANK = 256
B_NOPE = 128
B_ROPE = 64
B_V = 128
MOBA_BLOCK = 256
MOBA_TOPK = 3
D_HEAD_DIM = 64
D_HEADS = 8
D_KV_HEADS = 2
ROPE_THETA = 10000.0
RMS_EPS = 1e-6
LN_EPS = 1e-5
NEG = -1e30
DN_ALPHA = (2 * DEPTH) ** 0.25
SPLIT_SIZES = (512, 512, 512, 384, 256, 64, 512, 512, 512, 512, 128, 128, 2048)

LANE = 128
BLK = 128
ROWS = BATCH * SEQ

TN = 512
COL_AQ, COL_AK, COL_CQ, COL_CK, COL_DQ = 0, 512, 1024, 1536, 2048
COL_MIX = 2560
COL_DK, COL_KR, COL_CKV = 2560, 2688, 2816
COL_AV, COL_CV = 3072, 3584
COL_MIX2 = 4096
COL_BCQ, COL_DV = 4096, 4480
COL_GATE = 4608
N_PROJ = 6656

VMEM_LIMIT = 48 * 1024 * 1024


def _nt(a, b):
    return lax.dot_general(a, b, (((1,), (1,)), ((), ())), preferred_element_type=F32)


_OFFS = np.cumsum((0,) + SPLIT_SIZES)
(_O_AQ, _O_AK, _O_AV, _O_BCQ, _O_BCKV, _O_BKR, _O_CQ, _O_CK, _O_CV, _O_DQ, _O_DK, _O_DV, _O_GATE) = (int(o) for o in _OFFS[:-1])


def _src_rows():
    def run(start):
        return [start + LANE * g for g in range(TN // LANE)]

    tiles = [run(_O_AQ), run(_O_AK), run(_O_CQ), run(_O_CK), run(_O_DQ),
             [_O_DK, _O_BKR, _O_BCKV, _O_BCKV + LANE],
             run(_O_AV), run(_O_CV),
             [_O_BCQ, _O_BCQ + LANE, _O_BCQ + 2 * LANE, _O_DV]]
    tiles += [run(_O_GATE + TN * t) for t in range(4)]
    return np.asarray(tiles, np.int32).reshape(-1)


def _permute_w_uq(w_uq):
    L = w_uq.shape[0]
    w = w_uq.reshape(L, B_Q_RANK, B_HEADS, B_NOPE + B_ROPE)
    w = jnp.pad(w, ((0, 0), (0, 0), (0, 0), (0, 2 * LANE - B_NOPE - B_ROPE)))
    return w.reshape(L, B_Q_RANK, B_HEADS * 2 * LANE).astype(BF16)


def _rope_angles(dim):
    pos = jnp.arange(SEQ, dtype=jnp.int32)
    inv = ROPE_THETA ** (-jnp.arange(0, dim, 2, dtype=jnp.float32) / dim)
    ang = pos.astype(jnp.float32)[:, None] * inv[None, :]
    return jnp.cos(ang), jnp.sin(ang)


def _rope_tables():
    c, s = _rope_angles(HEAD_DIM)
    cos128 = jnp.concatenate([c, c], -1)
    sin128 = jnp.concatenate([-s, s], -1)
    c, s = _rope_angles(D_HEAD_DIM)
    z = jnp.zeros_like(s)
    cos64 = jnp.concatenate([c, c, c, c], -1)
    sin64a = jnp.concatenate([z, s, z, s], -1)
    sin64b = jnp.concatenate([-s, z, -s, z], -1)
    return cos128, sin128, cos64, sin64a, sin64b


TM_IN = 1024
IN_CHUNK = 256


def _rope128(x, c, s):
    return x * c + pltpu.roll(x, 64, 1) * s


def _rope64(x, c, sa, sb):
    return x * c + pltpu.roll(x, 32, 1) * sa + pltpu.roll(x, 96, 1) * sb


N_GRP = TN // LANE


def _inproj_body(tbl_ref, x_ref, w0, w1, w2, w3, c128, s128, c64, sa64, sb64, o_ref, wt_s):
    del tbl_ref
    j = pl.program_id(0)
    i = pl.program_id(1)

    @pl.when(i == 0)
    def _():
        for g, w in enumerate((w0, w1, w2, w3)):
            wt_s[g * LANE:(g + 1) * LANE, :] = w[0].astype(BF16)

        @pl.when(j == COL_MIX // TN)
        def _():
            wt_s[LANE + B_ROPE:2 * LANE, :] = jnp.zeros((LANE - B_ROPE, D_MODEL), BF16)

    pos0 = (i % (SEQ // TM_IN)) * TM_IN
    groups = [slice(g * LANE, (g + 1) * LANE) for g in range(N_GRP)]

    def row_chunks(epilogue):
        def piece(c):
            return _nt(x_ref[c * IN_CHUNK:(c + 1) * IN_CHUNK, :], wt_s[...])

        acc_next = piece(0)
        for c in range(TM_IN // IN_CHUNK):
            acc = acc_next
            if c + 1 < TM_IN // IN_CHUNK:
                acc_next = piece(c + 1)
            epilogue(acc, slice(c * IN_CHUNK, (c + 1) * IN_CHUNK),
                     pl.ds(pl.multiple_of(pos0 + c * IN_CHUNK, IN_CHUNK), IN_CHUNK))

    @pl.when(j < 4)
    def _():
        scale = jnp.where((j == 0) | (j == 2), HEAD_DIM ** -0.5, 1.0).astype(F32)

        def epilogue(acc, rows, pos):
            c, s = c128[pos, :], s128[pos, :]
            for sl in groups:
                o_ref[rows, sl] = (_rope128(acc[:, sl], c, s) * scale).astype(BF16)

        row_chunks(epilogue)

    @pl.when(j == 4)
    def _():
        def epilogue(acc, rows, pos):
            c, sa, sb = c64[pos, :], sa64[pos, :], sb64[pos, :]
            for sl in groups:
                o_ref[rows, sl] = (_rope64(acc[:, sl], c, sa, sb) * (D_HEAD_DIM ** -0.5)).astype(BF16)

        row_chunks(epilogue)

    @pl.when(j == 5)
    def _():
        def epilogue(acc, rows, pos):
            c, sa, sb = c64[pos, :], sa64[pos, :], sb64[pos, :]
            for sl in groups[:2]:
                o_ref[rows, sl] = _rope64(acc[:, sl], c, sa, sb).astype(BF16)
            o_ref[rows, 2 * LANE:] = acc[:, 2 * LANE:].astype(BF16)

        row_chunks(epilogue)

    @pl.when(j > 5)
    def _():
        def epilogue(acc, rows, pos):
            o_ref[rows, :] = acc.astype(BF16)

        row_chunks(epilogue)


def _inproj(xb, w_t, layer, tables):
    def w_spec(g):
        return pl.BlockSpec((pl.Element(1), pl.Element(LANE), pl.Element(D_MODEL)),
                            lambda j, i, tbl: (layer, pl.multiple_of(tbl[j * N_GRP + g], B_ROPE), 0))

    tab_spec = pl.BlockSpec((SEQ, LANE), lambda j, i, tbl: (0, 0))
    return pl.pallas_call(
        _inproj_body,
        grid_spec=pltpu.PrefetchScalarGridSpec(
            num_scalar_prefetch=1,
            grid=(N_PROJ // TN, ROWS // TM_IN),
            in_specs=[pl.BlockSpec((TM_IN, D_MODEL), lambda j, i, tbl: (i, 0))]
            + [w_spec(g) for g in range(N_GRP)] + [tab_spec] * 5,
            out_specs=pl.BlockSpec((TM_IN, TN), lambda j, i, tbl: (i, j)),
            scratch_shapes=[pltpu.VMEM((TN, D_MODEL), BF16)]),
        out_shape=jax.ShapeDtypeStruct((ROWS, N_PROJ), BF16),
        compiler_params=pltpu.CompilerParams(dimension_semantics=("arbitrary", "arbitrary"),
                                             vmem_limit_bytes=VMEM_LIMIT),
        name="inproj",
    )(jnp.asarray(_src_rows()), xb, w_t, w_t, w_t, w_t, *tables)


A_GROUP = 4


def _a_body(q_ref, k_ref, v_ref, o_ref, qf, kf, vf, m_s, l_s, acc_s):
    qf[...] = q_ref[...].astype(F32)
    kf[...] = k_ref[...].astype(F32)
    vf[...] = v_ref[...].astype(F32)
    row = lax.broadcasted_iota(jnp.int32, (BLK, BLK), 0)
    col = lax.broadcasted_iota(jnp.int32, (BLK, BLK), 1)
    own_ok = col <= row
    prev_ok = col >= row
    both_ok = jnp.concatenate([prev_ok, own_ok], axis=1)

    def rows(start, size, d):
        return pl.ds(start, size) if d == 1 else pl.ds(start, size, stride=d)

    def kv_rows(d, start, has_prev):
        return rows(start - d * BLK, 2 * BLK, d) if has_prev else rows(start, BLK, d)

    def scores(p, d, start, has_prev):
        q = qf[rows(start, BLK, d), :].astype(BF16)
        kk = kf[kv_rows(d, start, has_prev), :].astype(BF16)
        return jnp.where(both_ok if has_prev else own_ok, _nt(q, kk), NEG)

    def softmax(s):
        m = jnp.max(s, axis=-1, keepdims=True)
        e = jnp.exp(s - m)
        return e.astype(BF16), m, jnp.sum(e, axis=-1, keepdims=True)

    def finish(e, m, l, p, d, start, has_prev):
        vv = vf[kv_rows(d, start, has_prev), :].astype(BF16)
        dst = rows(start, BLK, d)
        m_s[p, dst, :] = jnp.broadcast_to(m, (BLK, LANE))
        l_s[p, dst, :] = jnp.broadcast_to(l, (BLK, LANE))
        acc_s[p, dst, :] = jnp.dot(e, vv, preferred_element_type=F32)

    blocks = [(p, d, r + d * BLK * n, n > 0)
              for p, (_, d) in enumerate(A_PATTERNS) for r in range(d) for n in range(SEQ // d // BLK)]
    groups = [blocks[i:i + A_GROUP] for i in range(0, len(blocks), A_GROUP)]
    s_next = [scores(*b) for b in groups[0]]
    for gi, grp in enumerate(groups):
        s_cur = s_next
        if gi + 1 < len(groups):
            s_next = [scores(*b) for b in groups[gi + 1]]
        soft = [softmax(s) for s in s_cur]
        for sf, b in zip(soft, grp):
            finish(*sf, *b)

    CH = 256

    def combine(c, carry):
        sl = pl.ds(pl.multiple_of(c * CH, CH), CH)
        ms = [m_s[p, sl, :] for p in range(3)]
        mx = jnp.maximum(jnp.maximum(ms[0], ms[1]), ms[2])
        ws = [jnp.exp(m - mx) for m in ms]
        num = ws[0] * acc_s[0, sl, :] + ws[1] * acc_s[1, sl, :] + ws[2] * acc_s[2, sl, :]
        den = ws[0] * l_s[0, sl, :] + ws[1] * l_s[1, sl, :] + ws[2] * l_s[2, sl, :]
        o_ref[sl, :] = (num / den).astype(BF16)
        return carry

    lax.fori_loop(0, SEQ // CH, combine, 0)


def _a_attn(proj):
    nh = BRANCH_W // HEAD_DIM

    def spec(col):
        return pl.BlockSpec((SEQ, HEAD_DIM), lambda b, h: (b, col // HEAD_DIM + h))

    return pl.pallas_call(
        _a_body,
        grid=(BATCH, nh),
        in_specs=[spec(COL_AQ), spec(COL_AK), spec(COL_AV)],
        out_specs=pl.BlockSpec((SEQ, HEAD_DIM), lambda b, h: (b, h)),
        out_shape=jax.ShapeDtypeStruct((ROWS, BRANCH_W), BF16),
        scratch_shapes=[pltpu.VMEM((SEQ, HEAD_DIM), F32)] * 3 + [pltpu.VMEM((3, SEQ, LANE), F32)] * 3,
        compiler_params=pltpu.CompilerParams(dimension_semantics=("arbitrary", "arbitrary"),
                                             vmem_limit_bytes=VMEM_LIMIT),
        name="mixer_a",
    )(proj, proj, proj)


QB = 256
NQB = SEQ // QB
CAUSAL_AHEAD = 3


def _rows(ref, i):
    return ref[i * QB:(i + 1) * QB, :]


def _causal_attend(q_ref, k_ref, vt_s, o_ref, bias_of):
    kk = lax.broadcasted_iota(jnp.int32, (QB, QB), 0)
    qq = lax.broadcasted_iota(jnp.int32, (QB, QB), 1)
    causal = kk <= qq
    order = list(range(NQB - 1, -1, -1))

    def scores(n):
        return _nt(k_ref[0:(n + 1) * QB, :], _rows(q_ref, n))

    pending = [scores(n) for n in order[:CAUSAL_AHEAD]]
    for i, n in enumerate(order):
        s_t = pending.pop(0)
        if i + CAUSAL_AHEAD < len(order):
            pending.append(scores(order[i + CAUSAL_AHEAD]))
        blocks = [s_t[j * QB:(j + 1) * QB] for j in range(n + 1)]
        blocks[n] = jnp.where(causal, blocks[n], NEG)
        for j in range(n):
            bias = bias_of(n, j)
            if bias is not None:
                blocks[j] = blocks[j] + bias
        m = functools.reduce(jnp.maximum, [jnp.max(b, axis=0, keepdims=True) for b in blocks])
        ps = [jnp.exp(b - m) for b in blocks]
        l = functools.reduce(jnp.add, [jnp.sum(p, axis=0, keepdims=True) for p in ps])
        p_all = jnp.concatenate([p.astype(BF16) for p in ps], axis=0)
        acc_t = jnp.dot(vt_s[:, 0:(n + 1) * QB], p_all, preferred_element_type=F32)
        o_ref[n * QB:(n + 1) * QB, :] = (acc_t * (1.0 / l)).T.astype(BF16)


TM_BP = 512


def _rms(x, g_ref):
    xf = x.astype(F32)
    return xf * lax.rsqrt(jnp.mean(xf * xf, -1, keepdims=True) + RMS_EPS) * g_ref[...]


def _bprep_body(mix_ref, mix2_ref, qn_ref, wuq_ref, kvn_ref, wukv_ref, c64, sa64, sb64, q_ref, k_ref, v_ref):
    cq = _rms(mix2_ref[:, :B_Q_RANK], qn_ref).astype(BF16)
    q = jnp.dot(cq, wuq_ref[...], preferred_element_type=F32)
    scale = (B_NOPE + B_ROPE) ** -0.5
    pos = pl.ds(pl.multiple_of((pl.program_id(0) % (SEQ // TM_BP)) * TM_BP, TM_BP), TM_BP)
    c, sa, sb = c64[pos, :], sa64[pos, :], sb64[pos, :]
    for h in range(B_HEADS):
        o = h * 2 * LANE
        q_ref[:, o:o + LANE] = (q[:, o:o + LANE] * scale).astype(BF16)
        q_ref[:, o + LANE:o + 2 * LANE] = (_rope64(q[:, o + LANE:o + 2 * LANE], c, sa, sb) * scale).astype(BF16)
    ckv = _rms(mix_ref[:, 2 * LANE:], kvn_ref).astype(BF16)
    kv = jnp.dot(ckv, wukv_ref[...], preferred_element_type=F32)
    kr = mix_ref[:, LANE:2 * LANE]
    for h in range(B_HEADS):
        o = h * 2 * LANE
        k_ref[:, o:o + LANE] = kv[:, o:o + LANE].astype(BF16)
        k_ref[:, o + LANE:o + 2 * LANE] = kr
        v_ref[:, h * LANE:(h + 1) * LANE] = kv[:, o + LANE:o + 2 * LANE].astype(BF16)


def _bprep(proj, q_norm, w_uq_p, kv_norm, w_ukv, tables):
    tab_spec = pl.BlockSpec((SEQ, LANE), lambda i: (0, 0))
    full = lambda shape: pl.BlockSpec(shape, lambda i: (0, 0))
    return pl.pallas_call(
        _bprep_body,
        grid=(ROWS // TM_BP,),
        in_specs=[pl.BlockSpec((TM_BP, TN), lambda i: (i, COL_MIX // TN)),
                  pl.BlockSpec((TM_BP, TN), lambda i: (i, COL_MIX2 // TN)),
                  full((1, B_Q_RANK)), full((B_Q_RANK, B_HEADS * 2 * LANE)),
                  full((1, B_KV_RANK)), full((B_KV_RANK, B_HEADS * 2 * LANE))] + [tab_spec] * 3,
        out_specs=[pl.BlockSpec((TM_BP, B_HEADS * 2 * LANE), lambda i: (i, 0)),
                   pl.BlockSpec((TM_BP, B_HEADS * 2 * LANE), lambda i: (i, 0)),
                   pl.BlockSpec((TM_BP, B_HEADS * LANE), lambda i: (i, 0))],
        out_shape=[jax.ShapeDtypeStruct((ROWS, B_HEADS * 2 * LANE), BF16),
                   jax.ShapeDtypeStruct((ROWS, B_HEADS * 2 * LANE), BF16),
                   jax.ShapeDtypeStruct((ROWS, B_HEADS * LANE), BF16)],
        compiler_params=pltpu.CompilerParams(dimension_semantics=("arbitrary",),
                                             vmem_limit_bytes=VMEM_LIMIT),
        name="mla_prep",
    )(proj, proj, q_norm, w_uq_p, kv_norm, w_ukv, *tables[2:])


def _store_vt(v_ref, vt_s):
    for j in range(NQB):
        vt_s[:, j * QB:(j + 1) * QB] = _rows(v_ref, j).astype(F32).T.astype(BF16)


def _battn_body(q_ref, k_ref, v_ref, o_ref, vt_s):
    _store_vt(v_ref, vt_s)
    _causal_attend(q_ref, k_ref, vt_s, o_ref, lambda n, j: None)


def _battn(qb, kb, vb):
    return pl.pallas_call(
        _battn_body,
        grid=(BATCH, B_HEADS),
        in_specs=[pl.BlockSpec((SEQ, 2 * LANE), lambda b, h: (b, h)),
                  pl.BlockSpec((SEQ, 2 * LANE), lambda b, h: (b, h)),
                  pl.BlockSpec((SEQ, B_V), lambda b, h: (b, h))],
        out_specs=pl.BlockSpec((SEQ, B_V), lambda b, h: (b, h)),
        out_shape=jax.ShapeDtypeStruct((ROWS, BRANCH_W), BF16),
        scratch_shapes=[pltpu.VMEM((B_V, SEQ), BF16)],
        compiler_params=pltpu.CompilerParams(dimension_semantics=("arbitrary",) * 2,
                                             vmem_limit_bytes=VMEM_LIMIT),
        name="mla_attn",
    )(qb, kb, vb)


KM_ROWS = 16


def _cattn_body(q_ref, k_ref, v_ref, o_ref, vt_s, km_s, bias_s):
    _store_vt(v_ref, vt_s)
    km_s[...] = jnp.zeros_like(km_s)
    for j in range(NQB):
        km_s[j:j + 1, :] = jnp.sum(_rows(k_ref, j).astype(F32), axis=0, keepdims=True) * (1.0 / MOBA_BLOCK)
    km = km_s[...]
    km_hi = km.astype(BF16)
    km_lo = (km - km_hi.astype(F32)).astype(BF16)
    jrow = lax.broadcasted_iota(jnp.int32, (KM_ROWS, QB), 0)
    for n in range(MOBA_TOPK + 1, NQB):
        q = _rows(q_ref, n)
        g_t = _nt(km_hi, q) + _nt(km_lo, q)
        cnt = jnp.zeros((KM_ROWS, QB), F32)
        for jp in range(n):
            gj = g_t[jp:jp + 1, :]
            beats = (gj > g_t) | ((gj == g_t) & (jp < jrow))
            cnt = cnt + jnp.where(beats, 1.0, 0.0)
        bias_s[n] = jnp.where(cnt < MOBA_TOPK, 0.0, NEG)

    def bias_of(n, j):
        return bias_s[n, j:j + 1, :] if n > MOBA_TOPK else None

    _causal_attend(q_ref, k_ref, vt_s, o_ref, bias_of)


def _cattn(proj):
    def spec(col):
        return pl.BlockSpec((SEQ, HEAD_DIM), lambda b, h: (b, col // HEAD_DIM + h))

    return pl.pallas_call(
        _cattn_body,
        grid=(BATCH, BRANCH_W // HEAD_DIM),
        in_specs=[spec(COL_CQ), spec(COL_CK), spec(COL_CV)],
        out_specs=pl.BlockSpec((SEQ, HEAD_DIM), lambda b, h: (b, h)),
        out_shape=jax.ShapeDtypeStruct((ROWS, BRANCH_W), BF16),
        scratch_shapes=[pltpu.VMEM((HEAD_DIM, SEQ), BF16), pltpu.VMEM((KM_ROWS, HEAD_DIM), F32),
                        pltpu.VMEM((NQB, KM_ROWS, QB), F32)],
        compiler_params=pltpu.CompilerParams(dimension_semantics=("arbitrary",) * 2,
                                             vmem_limit_bytes=VMEM_LIMIT),
        name="moba_attn",
    )(proj, proj, proj)


D_REP = D_HEADS // D_KV_HEADS
D_GROUP = 1


def _dattn_body(sink_ref, q_ref, k_ref, v_ref, o_ref, kk_s, vv_s):
    lane_full = lax.broadcasted_iota(jnp.int32, (SEQ, LANE), 1) < D_HEAD_DIM
    kf = k_ref[...].astype(F32)
    vf = v_ref[...].astype(F32)
    kr = pltpu.roll(kf, D_HEAD_DIM, 1)
    vr = pltpu.roll(vf, D_HEAD_DIM, 1)
    kk_s[0] = jnp.where(lane_full, kf, kr).astype(BF16)
    kk_s[1] = jnp.where(lane_full, kr, kf).astype(BF16)
    vv_s[0] = jnp.where(lane_full, vf, vr).astype(BF16)
    vv_s[1] = jnp.where(lane_full, vr, vf).astype(BF16)

    lo = lax.broadcasted_iota(jnp.int32, (BLK, LANE), 1) < D_HEAD_DIM
    row = lax.broadcasted_iota(jnp.int32, (BLK, BLK), 0)
    col = lax.broadcasted_iota(jnp.int32, (BLK, BLK), 1)
    own_ok = jnp.concatenate([col <= row] * D_REP, axis=0)
    prev_ok = jnp.concatenate([col > row] * D_REP, axis=0)
    both_ok = jnp.concatenate([prev_ok, own_ok], axis=1)
    hrow = lax.broadcasted_iota(jnp.int32, (D_REP * BLK, 1), 0) // BLK

    sinks = []
    for g in range(D_KV_HEADS):
        sink = jnp.zeros((D_REP * BLK, 1), F32)
        for i in range(D_REP):
            sink = jnp.where(hrow == i, sink_ref[g * D_REP + i], sink)
        sinks.append(sink)

    def kv_rows(n):
        return pl.ds((n - 1) * BLK, 2 * BLK) if n > 0 else pl.ds(0, BLK)

    def scores(g, n):
        zero = jnp.zeros((BLK, LANE), BF16)
        parts = []
        for pg in range(2):
            c0 = (2 * g + pg) * LANE
            qp = q_ref[n * BLK:(n + 1) * BLK, c0:c0 + LANE]
            parts += [jnp.where(lo, qp, zero), jnp.where(lo, zero, qp)]
        q4 = jnp.concatenate(parts, axis=0)
        return jnp.where(both_ok if n > 0 else own_ok, _nt(q4, kk_s[g, kv_rows(n), :]), NEG)

    def softmax(s, g, n):
        m2 = jnp.maximum(jnp.max(s, axis=-1, keepdims=True), sinks[g])
        e = jnp.exp(s - m2)
        den = jnp.sum(e, axis=-1, keepdims=True) + jnp.exp(sinks[g] - m2)
        return e.astype(BF16), 1.0 / den

    def finish(e, inv, g, n):
        o4 = jnp.dot(e, vv_s[g, kv_rows(n), :], preferred_element_type=F32) * inv
        for pg in range(2):
            c0 = (2 * g + pg) * LANE
            o_ref[n * BLK:(n + 1) * BLK, c0:c0 + LANE] = jnp.where(
                lo, o4[(2 * pg) * BLK:(2 * pg + 1) * BLK], o4[(2 * pg + 1) * BLK:(2 * pg + 2) * BLK]).astype(BF16)

    blocks = [(g, n) for g in range(D_KV_HEADS) for n in range(SEQ // BLK)]
    groups = [blocks[i:i + D_GROUP] for i in range(0, len(blocks), D_GROUP)]
    s_next = [scores(*b) for b in groups[0]]
    for gi, grp in enumerate(groups):
        s_cur = s_next
        if gi + 1 < len(groups):
            s_next = [scores(*b) for b in groups[gi + 1]]
        soft = [softmax(s, *b) for s, b in zip(s_cur, grp)]
        for sf, b in zip(soft, grp):
            finish(*sf, *b)


def _dattn(proj, sinks):
    return pl.pallas_call(
        _dattn_body,
        grid=(BATCH,),
        in_specs=[pl.BlockSpec(memory_space=pltpu.SMEM),
                  pl.BlockSpec((SEQ, BRANCH_W), lambda b: (b, COL_DQ // BRANCH_W)),
                  pl.BlockSpec((SEQ, LANE), lambda b: (b, COL_DK // LANE)),
                  pl.BlockSpec((SEQ, LANE), lambda b: (b, COL_DV // LANE))],
        out_specs=pl.BlockSpec((SEQ, BRANCH_W), lambda b: (b, 0)),
        out_shape=jax.ShapeDtypeStruct((ROWS, BRANCH_W), BF16),
        scratch_shapes=[pltpu.VMEM((D_KV_HEADS, SEQ, LANE), BF16)] * 2,
        compiler_params=pltpu.CompilerParams(dimension_semantics=("arbitrary",),
                                             vmem_limit_bytes=VMEM_LIMIT),
        name="sink_swa",
    )(sinks, proj, proj, proj)


TM_OUT = 512
OUT_CHUNK = 256


def _outproj_body(ya, yb, yc, yd, ga, gb, gc, gd, x_ref, w_ref, bn_ref, g_ref, b_ref, of_ref, ob_ref):
    for c in range(TM_OUT // OUT_CHUNK):
        rows = slice(c * OUT_CHUNK, (c + 1) * OUT_CHUNK)
        acc = DN_ALPHA * x_ref[rows, :]
        for i, (y_ref, gate_ref) in enumerate(((ya, ga), (yb, gb), (yc, gc), (yd, gd))):
            sl = slice(i * BRANCH_W, (i + 1) * BRANCH_W)
            y = y_ref[rows, :].astype(F32)
            y = y * lax.rsqrt(jnp.mean(y * y, -1, keepdims=True) + RMS_EPS) * bn_ref[:, sl]
            gt = gate_ref[rows, :].astype(F32)
            half = 0.5 * gt
            y = (y * (half + half * jnp.tanh(half))).astype(BF16)
            acc = acc + jnp.dot(y, w_ref[sl, :], preferred_element_type=F32)
        mu = jnp.mean(acc, -1, keepdims=True)
        xc = acc - mu
        var = jnp.mean(xc * xc, -1, keepdims=True)
        out = xc * lax.rsqrt(var + LN_EPS) * g_ref[...] + b_ref[...]
        of_ref[rows, :] = out
        ob_ref[rows, :] = out.astype(BF16)


def _outproj(ya, yb, yc, yd, proj, xf, w_out_all, layer, bn, ln_g, ln_b):
    y_spec = pl.BlockSpec((TM_OUT, BRANCH_W), lambda i: (i, 0))
    gate_specs = [pl.BlockSpec((TM_OUT, BRANCH_W), functools.partial(lambda i, c: (i, c), c=COL_GATE // BRANCH_W + g))
                  for g in range(4)]
    row_spec = pl.BlockSpec((1, D_MODEL), lambda i: (0, 0))
    x_spec = pl.BlockSpec((TM_OUT, D_MODEL), lambda i: (i, 0))
    return pl.pallas_call(
        _outproj_body,
        grid=(ROWS // TM_OUT,),
        in_specs=[y_spec] * 4 + gate_specs + [x_spec, pl.BlockSpec((None, D_MODEL, D_MODEL), lambda i: (layer, 0, 0)),
                                              row_spec, row_spec, row_spec],
        out_specs=[x_spec, x_spec],
        out_shape=[jax.ShapeDtypeStruct((ROWS, D_MODEL), F32), jax.ShapeDtypeStruct((ROWS, D_MODEL), BF16)],
        compiler_params=pltpu.CompilerParams(dimension_semantics=("arbitrary",),
                                             vmem_limit_bytes=VMEM_LIMIT),
        name="outproj",
    )(ya, yb, yc, yd, proj, proj, proj, proj, xf, w_out_all, bn, ln_g, ln_b)


def kernel(x, w_in, q_norm, w_uq, kv_norm, w_ukv, sinks, branch_norm, w_out, ln_gamma, ln_beta):
    assert x.shape == (BATCH, SEQ, D_MODEL) and w_in.shape == (DEPTH, D_MODEL, sum(SPLIT_SIZES))
    tables = _rope_tables()
    w_t = jnp.swapaxes(w_in, 1, 2)
    w_uq_p = _permute_w_uq(w_uq)
    w_ukv_b = w_ukv.astype(BF16)
    w_out_b = w_out.astype(BF16)
    xf = x.reshape(ROWS, D_MODEL)
    xb = xf.astype(BF16)
    for l in range(DEPTH):
        proj = _inproj(xb, w_t, l, tables)
        ya = _a_attn(proj)
        qb, kb, vb = _bprep(proj, q_norm[l][None], w_uq_p[l], kv_norm[l][None], w_ukv_b[l], tables)
        yb = _battn(qb, kb, vb)
        yc = _cattn(proj)
        yd = _dattn(proj, sinks[l])
        xf, xb = _outproj(ya, yb, yc, yd, proj, xf, w_out_b, l, branch_norm[l][None], ln_gamma[l][None], ln_beta[l][None])
    return xf.reshape(BATCH, SEQ, D_MODEL)
```
